```python
import math
import jax, jax.numpy as jnp
from jax import lax
import numpy as np

D_MODEL = 1024
BATCH = 16
SEQ = 2048
DEPTH = 2

GRID_W = 64
CTX_LEN = 256
N_MOD = 9
FFN_HIDDEN = 2816
LN_EPS = 1e-5
RMS_EPS = 1e-6
DEEPNORM_ALPHA = (2 * DEPTH) ** 0.25
DEEPNORM_BETA = (8 * DEPTH) ** -0.25
MIX_WIDTH = 1024

RWKV_HEADS = 8
RWKV_HEAD_DIM = 64
RWKV_WIDTH = RWKV_HEADS * RWKV_HEAD_DIM
RWKV_DECAY_LORA = 64
RWKV_AAA_LORA = 64
RWKV_GATE_LORA = 128
RWKV_GN_EPS = 64e-5
NA_HEADS = 8
NA_HEAD_DIM = 64
NA_WIDTH = NA_HEADS * NA_HEAD_DIM
NA_WIN_R = 8
NA_WIN_C = 16
GLA_HEADS = 4
GLA_DK = 64
GLA_DV = 128
GLA_GATE_LORA = 16
GLA_GATE_NORM = 16.0
GLA_CHUNK = 64
DIFF_HEADS = 4
DIFF_DH = 64
DIFF_DV = 128
Q_BLOCK = 128
ROPE_BASE = 10000.0

RWKV_COLS = (RWKV_WIDTH, RWKV_WIDTH, RWKV_WIDTH, RWKV_DECAY_LORA, RWKV_DECAY_LORA, RWKV_AAA_LORA, RWKV_AAA_LORA, RWKV_GATE_LORA)
NA_COLS = (NA_WIDTH, NA_WIDTH, NA_WIDTH)
GLA_COLS = (GLA_HEADS * GLA_DK, GLA_HEADS * GLA_DK, GLA_HEADS * GLA_DV, GLA_HEADS * GLA_DV, GLA_GATE_LORA, GLA_GATE_LORA)
DIFF_COLS = (DIFF_HEADS * 2 * DIFF_DH, DIFF_HEADS * 2 * DIFF_DH, DIFF_HEADS * DIFF_DV)
IN_COLS_EVEN = sum(RWKV_COLS) + sum(NA_COLS)
IN_COLS_ODD = sum(GLA_COLS) + sum(DIFF_COLS)

kernel_name = 'hybrid_rwkv7_natten_gla_diffattn_dit'


def split_cols(u, sizes):
    out, o = [], 0
    for s in sizes:
        out.append(u[..., o:o + s])
        o += s
    return out


def layer_norm(x, g, b):
    xf = x.astype(jnp.float32)
    mu = jnp.mean(xf, -1, keepdims=True)
    var = jnp.mean(jnp.square(xf - mu), -1, keepdims=True)
    return ((xf - mu) * lax.rsqrt(var + LN_EPS) * g + b).astype(x.dtype)


def head_rms_norm(x, g):
    xf = x.astype(jnp.float32)
    return (xf * lax.rsqrt(jnp.mean(xf * xf, -1, keepdims=True) + RMS_EPS) * g).astype(x.dtype)


def to_heads(t, n_heads):
    b, l, _ = t.shape
    return t.reshape(b, l, n_heads, -1).transpose(0, 2, 1, 3)


def from_heads(t):
    b, h, l, d = t.shape
    return t.transpose(0, 2, 1, 3).reshape(b, l, h * d)


def modulate(x, shift, scale):
    return x * (1 + scale) + shift


def post_norm(x, y, gate, g, b):
    return layer_norm(DEEPNORM_ALPHA * x + gate * y, g, b)


def swiglu(h, w_gu, w_down):
    gate, up = jnp.split(h @ w_gu, 2, axis=-1)
    return (jax.nn.silu(gate) * up) @ w_down


def half_ffn_update(x, mods, w_gu, w_down, g, b):
    shift, scale, gate = mods
    return post_norm(x, 0.5 * swiglu(modulate(x, shift, scale), w_gu, w_down), gate, g, b)


def _rotate(x, pos):
    n = x.shape[-1] // 2
    freqs = ROPE_BASE ** (-jnp.arange(n, dtype=jnp.float32) / n)
    ang = pos.astype(jnp.float32)[:, None] * freqs[None, :]
    cos = jnp.cos(ang).astype(x.dtype)
    sin = jnp.sin(ang).astype(x.dtype)
    x1, x2 = x[..., :n], x[..., n:]
    return jnp.concatenate([x1 * cos - x2 * sin, x2 * cos + x1 * sin], axis=-1)


def axial_rope(t, pos_row, pos_col):
    half = t.shape[-1] // 2
    return jnp.concatenate([_rotate(t[..., :half], pos_row), _rotate(t[..., half:], pos_col)], axis=-1)


def centred_token_shift(u, mu):
    prev = jnp.pad(u[:, :-1], ((0, 0), (1, 0), (0, 0)))
    nxt = jnp.pad(u[:, 1:], ((0, 0), (0, 1), (0, 0)))
    return u + mu * (0.5 * (prev + nxt) - u)


def rwkv7_inputs(u, mu, w0, w2, a0, a2, k_k, k_a):
    f32 = jnp.float32
    u = centred_token_shift(u, mu)
    r, k, v, wd_f, wd_b, ad_f, ad_b, gd = split_cols(u, RWKV_COLS)
    b_, t_, _ = u.shape
    heads = lambda z: z.astype(f32).reshape(b_, t_, RWKV_HEADS, RWKV_HEAD_DIM)

    def decay(wd, w0_d, w2_d):
        wl = (w0_d + jnp.tanh(wd) @ w2_d).astype(f32)
        return jnp.exp(-jnp.exp(-jax.nn.softplus(-wl) - 0.5))

    def iclr(ad, a0_d, a2_d):
        return jax.nn.sigmoid((a0_d + ad @ a2_d).astype(f32))

    a_f, a_b = iclr(ad_f, a0[0], a2[0]), iclr(ad_b, a0[1], a2[1])
    kf = k.astype(f32)
    kk = heads(kf * k_k)
    kk = kk / jnp.maximum(jnp.sqrt(jnp.sum(kk * kk, -1, keepdims=True)), 1e-12)
    return dict(r=heads(r), k=heads(k), v=heads(v), kk=kk,
                w_f=heads(decay(wd_f, w0[0], w2[0])), w_b=heads(decay(wd_b, w0[1], w2[1])),
                k_f=heads(kf * (1 + (a_f - 1) * k_a)), k_b=heads(kf * (1 + (a_b - 1) * k_a)),
                a_f=heads(a_f), a_b=heads(a_b), gd=gd)


def rwkv7_scan(state, r, w, k, v, kk, a, reverse, emit):
    def step(S, inp):
        r_t, w_t, k_t, v_t, kk_t, a_t = inp
        s_kk = jnp.einsum('bhvk,bhk->bhv', S, kk_t)
        S = S * w_t[:, :, None, :] - s_kk[..., None] * (kk_t * a_t)[:, :, None, :] + v_t[..., None] * k_t[:, :, None, :]
        if not emit:
            return S, None
        return S, jnp.einsum('bhvk,bhk->bhv', S, r_t)
    xs = tuple(jnp.moveaxis(t, 1, 0) for t in (r, w, k, v, kk, a))
    S, y = lax.scan(step, state, xs, reverse=reverse)
    return S, (jnp.moveaxis(y, 0, 1) if emit else None)


def rwkv7_output(y, d, g2, gn_g, gn_b, r_k, out_dtype):
    b_, t_ = y.shape[:2]
    mu = jnp.mean(y, -1, keepdims=True)
    var = jnp.mean(jnp.square(y - mu), -1, keepdims=True)
    yn = ((y - mu) * lax.rsqrt(var + RWKV_GN_EPS)).reshape(b_, t_, RWKV_WIDTH) * gn_g + gn_b
    bonus = (jnp.sum(d['r'] * d['k'] * r_k, -1, keepdims=True) * d['v']).reshape(b_, t_, RWKV_WIDTH)
    g = jax.nn.sigmoid(d['gd']) @ g2
    return ((yn + bonus) * g).astype(out_dtype)


def rwkv7_mix(u, uc, mu, w0, w2, a0, a2, g2, k_k, k_a, r_k, gn_g, gn_b, need_ctx):
    lat = rwkv7_inputs(u, mu, w0, w2, a0, a2, k_k, k_a)
    cx = rwkv7_inputs(uc, mu, w0, w2, a0, a2, k_k, k_a)
    S0 = jnp.zeros((u.shape[0], RWKV_HEADS, RWKV_HEAD_DIM, RWKV_HEAD_DIM), jnp.float32)

    def run(d, S, dr, reverse, emit):
        return rwkv7_scan(S, d['r'], d['w_' + dr], d['k_' + dr], d['v'], d['kk'], d['a_' + dr], reverse, emit)

    S_f, yc_f = run(cx, S0, 'f', False, need_ctx)
    S_b, yc_b = run(cx, S0, 'b', True, need_ctx)
    _, y_f = run(lat, S_f, 'f', False, True)
    _, y_b = run(lat, S_b, 'b', True, True)
    y = rwkv7_output(y_f + y_b, lat, g2, gn_g, gn_b, r_k, u.dtype)
    yc = rwkv7_output(yc_f + yc_b, cx, g2, gn_g, gn_b, r_k, uc.dtype) if need_ctx else None
    return y, yc


def dense_ctx_attention(q, k, v):
    p = jax.nn.softmax(jnp.einsum('bhqd,bhkd->bhqk', q, k).astype(jnp.float32), -1).astype(v.dtype)
    return jnp.einsum('bhqk,bhkd->bhqd', p, v)


def neighbourhood_attention(q, k, v, qc, kc, vc, rpb, need_ctx):
    b_, s_, _ = q.shape
    rows = s_ // GRID_W
    wr, wc = min(NA_WIN_R, rows), NA_WIN_C
    H, dh = NA_HEADS, NA_HEAD_DIM
    scale = dh ** -0.5
    grid = lambda t: t.reshape(b_, rows, GRID_W, H, dh).transpose(0, 3, 1, 2, 4)
    qg, kg, vg = grid(q * scale), grid(k), grid(v)
    kch, vch = to_heads(kc, H), to_heads(vc, H)
    cols = jnp.arange(GRID_W)
    col_start = jnp.clip(cols - wc // 2, 0, GRID_W - wc)
    col_in = (cols[None, :] >= col_start[:, None]) & (cols[None, :] < col_start[:, None] + wc)
    dc_idx = jnp.clip(cols[None, :] - cols[:, None] + NA_WIN_C - 1, 0, 2 * NA_WIN_C - 2)
    rpb_cols = rpb[:, :, dc_idx]

    def one_row(args):
        r, q_r = args
        rs = jnp.clip(r - wr // 2, 0, rows - wr)
        k_band = lax.dynamic_slice_in_dim(kg, rs, wr, axis=2)
        v_band = lax.dynamic_slice_in_dim(vg, rs, wr, axis=2)
        dr_idx = rs + jnp.arange(wr) - r + NA_WIN_R - 1
        bias = jnp.take(rpb_cols, dr_idx, axis=1).transpose(0, 2, 1, 3)
        s_loc = jnp.einsum('bhqd,bhikd->bhqik', q_r, k_band).astype(jnp.float32) + bias[None]
        s_loc = jnp.where(col_in[:, None, :], s_loc, -jnp.inf).reshape(b_, H, GRID_W, wr * GRID_W)
        s_ctx = jnp.einsum('bhqd,bhld->bhql', q_r, kch).astype(jnp.float32)
        p = jax.nn.softmax(jnp.concatenate([s_loc, s_ctx], -1), -1).astype(v.dtype)
        p_loc = p[..., :wr * GRID_W].reshape(b_, H, GRID_W, wr, GRID_W)
        p_ctx = p[..., wr * GRID_W:]
        return jnp.einsum('bhqik,bhikd->bhqd', p_loc, v_band) + jnp.einsum('bhql,bhld->bhqd', p_ctx, vch)

    out = lax.map(one_row, (jnp.arange(rows), jnp.moveaxis(qg, 2, 0)))
    y = out.transpose(1, 0, 3, 2, 4).reshape(b_, s_, H * dh)
    yc = from_heads(dense_ctx_attention(to_heads(qc * scale, H), kch, vch)) if need_ctx else None
    return y, yc


def rwkv_na_mixer(h, hc, w_in, w_out, mu, w0, w2, a0, a2, g2, k_k, k_a, r_k, gn_g, gn_b, rpb, need_ctx):
    u, uc = h @ w_in, hc @ w_in
    n_a = sum(RWKV_COLS)
    y_a, yc_a = rwkv7_mix(u[..., :n_a], uc[..., :n_a], mu, w0, w2, a0, a2, g2, k_k, k_a, r_k, gn_g, gn_b, need_ctx)
    q, k, v = split_cols(u[..., n_a:], NA_COLS)
    qc, kc, vc = split_cols(uc[..., n_a:], NA_COLS)
    y_b, yc_b = neighbourhood_attention(q, k, v, qc, kc, vc, rpb, need_ctx)
    y = jnp.concatenate([y_a, y_b], -1) @ w_out
    yc = jnp.concatenate([yc_a, yc_b], -1) @ w_out if need_ctx else None
    return y, yc


def gla_chunk_scan(state, q, k, v, g, emit):
    b_, h_, t_, _ = q.shape
    n = t_ // GLA_CHUNK
    chunk = lambda t: t.reshape(b_, h_, n, GLA_CHUNK, t.shape[-1]).transpose(2, 0, 1, 3, 4)
    tri = jnp.tril(jnp.ones((GLA_CHUNK, GLA_CHUNK), bool))

    def step(S, inp):
        q_c, k_c, v_c, g_c = inp
        bcum = jnp.cumsum(g_c, axis=2)
        b_last = bcum[:, :, -1:, :]
        S_new = jnp.exp(b_last[:, :, 0, :, None]) * S + jnp.einsum('bhsk,bhsv->bhkv', k_c * jnp.exp(b_last - bcum), v_c)
        if not emit:
            return S_new, None
        o_inter = jnp.einsum('bhtk,bhkv->bhtv', q_c * jnp.exp(bcum), S)
        dec = jnp.exp(jnp.where(tri[:, :, None], bcum[:, :, :, None, :] - bcum[:, :, None, :, :], -jnp.inf))
        a_ts = jnp.einsum('bhtk,bhsk,bhtsk->bhts', q_c, k_c, dec)
        return S_new, o_inter + jnp.einsum('bhts,bhsv->bhtv', a_ts, v_c)

    S, o = lax.scan(step, state, tuple(chunk(t) for t in (q, k, v, g)))
    return S, (o.transpose(1, 2, 0, 3, 4).reshape(b_, h_, t_, -1) if emit else None)


def gla_mix(u, uc, g2, gb, norm_g, need_ctx):
    f32 = jnp.float32

    def prep(z):
        q, k, v, r, gd_f, gd_b = split_cols(z, GLA_COLS)
        heads = lambda t: to_heads(t.astype(f32), GLA_HEADS)
        gate = lambda gd, d: heads(jax.nn.log_sigmoid((gd @ g2[d] + gb[d]).astype(f32)) / GLA_GATE_NORM)
        return dict(q=heads(q) * GLA_DK ** -0.5, k=heads(k), v=heads(v), g_f=gate(gd_f, 0), g_b=gate(gd_b, 1), r=r)

    lat, cx = prep(u), prep(uc)
    S0 = jnp.zeros((u.shape[0], GLA_HEADS, GLA_DK, GLA_DV), f32)
    flip = lambda t: jnp.flip(t, axis=2)

    def run(d, S, dr, emit):
        if dr == 'f':
            return gla_chunk_scan(S, d['q'], d['k'], d['v'], d['g_f'], emit)
        S, o = gla_chunk_scan(S, flip(d['q']), flip(d['k']), flip(d['v']), flip(d['g_b']), emit)
        return S, (flip(o) if emit else None)

    def finish(o, r):
        return from_heads(head_rms_norm(o, norm_g)).astype(r.dtype) * jax.nn.silu(r)

    S_f, oc_f = run(cx, S0, 'f', need_ctx)
    S_b, oc_b = run(cx, S0, 'b', need_ctx)
    _, o_f = run(lat, S_f, 'f', True)
    _, o_b = run(lat, S_b, 'b', True)
    y = finish(o_f + o_b, lat['r'])
    yc = finish(oc_f + oc_b, cx['r']) if need_ctx else None
    return y, yc


def diff_attention(q, k, v, qc, kc, vc, lam_params, norm_g, lambda_init, need_ctx):
    b_, s_, _ = q.shape
    H = DIFF_HEADS
    scale = DIFF_DH ** -0.5
    pos = jnp.arange(s_)
    pos_r, pos_c = pos // GRID_W, pos % GRID_W
    pair = lambda t: t.reshape(t.shape[0], t.shape[1], H, 2, DIFF_DH).transpose(0, 2, 3, 1, 4)
    qh = axial_rope(pair(q), pos_r, pos_c) * scale
    kh = axial_rope(pair(k), pos_r, pos_c)
    vh = to_heads(v, H)
    kch, vch = pair(kc), to_heads(vc, H)
    lp = lam_params.astype(jnp.float32)
    lam = jnp.exp(jnp.sum(lp[0] * lp[1])) - jnp.exp(jnp.sum(lp[2] * lp[3])) + lambda_init

    def attend(q_blk, k_, v_):
        p = jax.nn.softmax(jnp.einsum('bhmqd,bhmkd->bhmqk', q_blk, k_).astype(jnp.float32), -1)
        a = p[:, :, 0] - lam * p[:, :, 1]
        return jnp.einsum('bhqk,bhkv->bhqv', a.astype(v_.dtype), v_)

    def finish(o):
        return from_heads(head_rms_norm(o, norm_g) * (1 - lambda_init))

    k_all = jnp.concatenate([kh, kch], axis=3)
    v_all = jnp.concatenate([vh, vch], axis=2)
    nb = s_ // Q_BLOCK
    q_blocks = qh.reshape(b_, H, 2, nb, Q_BLOCK, DIFF_DH).transpose(3, 0, 1, 2, 4, 5)
    o = lax.map(lambda qb: attend(qb, k_all, v_all), q_blocks)
    y = finish(o.transpose(1, 2, 0, 3, 4).reshape(b_, H, s_, DIFF_DV))
    yc = finish(attend(pair(qc) * scale, kch, vch)) if need_ctx else None
    return y, yc


def gla_diff_mixer(h, hc, w_in, w_out, gla_g2, gla_gb, gla_norm_g, diff_lambda, diff_norm_g, lambda_init, need_ctx):
    u, uc = h @ w_in, hc @ w_in
    n_c = sum(GLA_COLS)
    y_c, yc_c = gla_mix(u[..., :n_c], uc[..., :n_c], gla_g2, gla_gb, gla_norm_g, need_ctx)
    q, k, v = split_cols(u[..., n_c:], DIFF_COLS)
    qc, kc, vc = split_cols(uc[..., n_c:], DIFF_COLS)
    y_d, yc_d = diff_attention(q, k, v, qc, kc, vc, diff_lambda, diff_norm_g, lambda_init, need_ctx)
    y = jnp.concatenate([y_c, y_d], -1) @ w_out
    yc = jnp.concatenate([yc_c, yc_d], -1) @ w_out if need_ctx else None
    return y, yc


def setup_inputs(seed: int = 0) -> dict:
    key = jax.random.key(seed)
    ks = iter(jax.random.split(key, 64))
    nrm = lambda shape, s: jax.random.normal(next(ks), shape, jnp.float32) * s
    D = D_MODEL
    inp = {}
    inp['x'] = nrm((BATCH, SEQ, D), 1.0)
    inp['c'] = nrm((BATCH, D), 1.0)
    inp['ctx'] = nrm((BATCH, CTX_LEN, D), 1.0)
    inp['c_ctx'] = nrm((D,), 1.0)
    for l in range(DEPTH):
        inp[f'w_mod_{l}'] = nrm((D, N_MOD * D), 0.5 * D ** -0.5)
        inp[f'b_mod_{l}'] = nrm((N_MOD * D,), 0.02)
        inp[f'ln_g_{l}'] = 1.0 + nrm((3, D), 0.02)
        inp[f'ln_b_{l}'] = nrm((3, D), 0.02)
        inp[f'ffn_gu_{l}'] = nrm((2, D, 2 * FFN_HIDDEN), D ** -0.5)
        inp[f'ffn_down_{l}'] = nrm((2, FFN_HIDDEN, D), FFN_HIDDEN ** -0.5 * DEEPNORM_BETA)
        inp[f'w_in_{l}'] = nrm((D, IN_COLS_EVEN if l % 2 == 0 else IN_COLS_ODD), D ** -0.5)
        inp[f'w_out_{l}'] = nrm((MIX_WIDTH, D), MIX_WIDTH ** -0.5 * DEEPNORM_BETA)
        if l % 2 == 0:
            inp[f'rwkv_mu_{l}'] = jax.random.uniform(next(ks), (sum(RWKV_COLS),), jnp.float32)
            inp[f'rwkv_w0_{l}'] = jnp.linspace(-6.5, -1.5, RWKV_WIDTH, dtype=jnp.float32)[None] + nrm((2, RWKV_WIDTH), 0.1)
            inp[f'rwkv_w2_{l}'] = nrm((2, RWKV_DECAY_LORA, RWKV_WIDTH), 0.1 * RWKV_DECAY_LORA ** -0.5)
            inp[f'rwkv_a0_{l}'] = nrm((2, RWKV_WIDTH), 0.1)
            inp[f'rwkv_a2_{l}'] = nrm((2, RWKV_AAA_LORA, RWKV_WIDTH), 0.1 * RWKV_AAA_LORA ** -0.5)
            inp[f'rwkv_g2_{l}'] = nrm((RWKV_GATE_LORA, RWKV_WIDTH), RWKV_GATE_LORA ** -0.5)
            inp[f'rwkv_k_k_{l}'] = 0.85 + nrm((RWKV_WIDTH,), 0.02)
            inp[f'rwkv_k_a_{l}'] = 1.0 + nrm((RWKV_WIDTH,), 0.02)
            inp[f'rwkv_r_k_{l}'] = nrm((RWKV_HEADS, RWKV_HEAD_DIM), 0.1)
            inp[f'rwkv_gn_g_{l}'] = 1.0 + nrm((RWKV_WIDTH,), 0.02)
            inp[f'rwkv_gn_b_{l}'] = nrm((RWKV_WIDTH,), 0.02)
            inp[f'na_rpb_{l}'] = nrm((NA_HEADS, 2 * NA_WIN_R - 1, 2 * NA_WIN_C - 1), 0.1)
        else:
            inp[f'gla_g2_{l}'] = nrm((2, GLA_GATE_LORA, GLA_HEADS * GLA_DK), GLA_GATE_LORA ** -0.5)
            inp[f'gla_gb_{l}'] = nrm((2, GLA_HEADS * GLA_DK), 0.1)
            inp[f'gla_norm_g_{l}'] = 1.0 + nrm((GLA_DV,), 0.02)
            inp[f'diff_lambda_{l}'] = nrm((4, DIFF_DH), 0.1)
            inp[f'diff_norm_g_{l}'] = 1.0 + nrm((DIFF_DV,), 0.02)
    return inp


def reference(x, c, ctx, c_ctx,
              w_mod_0, b_mod_0, ln_g_0, ln_b_0, ffn_gu_0, ffn_down_0, w_in_0, w_out_0,
              rwkv_mu_0, rwkv_w0_0, rwkv_w2_0, rwkv_a0_0, rwkv_a2_0, rwkv_g2_0, rwkv_k_k_0, rwkv_k_a_0,
              rwkv_r_k_0, rwkv_gn_g_0, rwkv_gn_b_0, na_rpb_0,
              w_mod_1, b_mod_1, ln_g_1, ln_b_1, ffn_gu_1, ffn_down_1, w_in_1, w_out_1,
              gla_g2_1, gla_gb_1, gla_norm_g_1, diff_lambda_1, diff_norm_g_1):
    per_layer = (
        (w_mod_0, b_mod_0, ln_g_0, ln_b_0, ffn_gu_0, ffn_down_0, w_in_0, w_out_0),
        (w_mod_1, b_mod_1, ln_g_1, ln_b_1, ffn_gu_1, ffn_down_1, w_in_1, w_out_1),
    )
    mixer_params = (
        (rwkv_mu_0, rwkv_w0_0, rwkv_w2_0, rwkv_a0_0, rwkv_a2_0, rwkv_g2_0, rwkv_k_k_0, rwkv_k_a_0,
         rwkv_r_k_0, rwkv_gn_g_0, rwkv_gn_b_0, na_rpb_0),
        (gla_g2_1, gla_gb_1, gla_norm_g_1, diff_lambda_1, diff_norm_g_1),
    )
    b_ = x.shape[0]
    xl, xc = x, ctx
    for l in range(DEPTH):
        w_mod, b_mod, ln_g, ln_b, ffn_gu, ffn_down, w_in, w_out = per_layer[l]
        need_ctx = l < DEPTH - 1
        m = (jax.nn.silu(c) @ w_mod + b_mod).reshape(b_, N_MOD, 1, D_MODEL)
        mc = (jax.nn.silu(c_ctx) @ w_mod + b_mod).reshape(N_MOD, 1, 1, D_MODEL)
        ml = [m[:, i] for i in range(N_MOD)]
        mcx = [mc[i] for i in range(N_MOD)]
        xl = half_ffn_update(xl, ml[0:3], ffn_gu[0], ffn_down[0], ln_g[0], ln_b[0])
        xc = half_ffn_update(xc, mcx[0:3], ffn_gu[0], ffn_down[0], ln_g[0], ln_b[0])
        h = modulate(xl, ml[3], ml[4])
        hc = modulate(xc, mcx[3], mcx[4])
        if l % 2 == 0:
            y, yc = rwkv_na_mixer(h, hc, w_in, w_out, *mixer_params[l], need_ctx=need_ctx)
        else:
            y, yc = gla_diff_mixer(h, hc, w_in, w_out, *mixer_params[l],
                                   lambda_init=0.8 - 0.6 * math.exp(-0.3 * l), need_ctx=need_ctx)
        xl = post_norm(xl, y, ml[5], ln_g[1], ln_b[1])
        if need_ctx:
            xc = post_norm(xc, yc, mcx[5], ln_g[1], ln_b[1])
        xl = half_ffn_update(xl, ml[6:9], ffn_gu[1], ffn_down[1], ln_g[2], ln_b[2])
        if need_ctx:
            xc = half_ffn_update(xc, mcx[6:9], ffn_gu[1], ffn_down[1], ln_g[2], ln_b[2])
    return xl
```

```python
import functools
import math

import jax
import jax.numpy as jnp
from jax import lax
from jax.experimental import pallas as pl
from jax.experimental.pallas import tpu as pltpu

F32 = jnp.float32
BF16 = jnp.bfloat16

GRID_W = 64
N_MOD = 9
DEPTH = 2
LN_EPS = 1e-5
RMS_EPS = 1e-6
DEEPNORM_ALPHA = (2 * DEPTH) ** 0.25
RWKV_HEADS = 8
RWKV_HEAD_DIM = 64
RWKV_WIDTH = RWKV_HEADS * RWKV_HEAD_DIM
RWKV_IN = 3 * RWKV_WIDTH + 4 * 64 + 128
RWKV_GN_EPS = 64e-5
NA_HEADS = 8
NA_HEAD_DIM = 64
NA_WIN_R = 8
NA_WIN_C = 16
GLA_HEADS = 4
GLA_DK = 64
GLA_DV = 128
GLA_GATE_LORA = 16
GLA_GATE_NORM = 16.0
GLA_SUB = 16
DIFF_HEADS = 4
DIFF_DH = 64
DIFF_DV = 128
ROPE_BASE = 10000.0

LANES = 128
SUBLANES = 8
V7X_VMEM_BYTES = 64 * 1024 * 1024
VMEM_LIMIT = (V7X_VMEM_BYTES * 7) // 8


def _cparams(*sem):
    return pltpu.CompilerParams(dimension_semantics=sem, vmem_limit_bytes=VMEM_LIMIT)


def _mm(a, b):
    return jnp.dot(a.astype(BF16), b.astype(BF16), preferred_element_type=F32)


def _mm_nt(a, b):
    return lax.dot_general(a.astype(BF16), b.astype(BF16), (((1,), (1,)), ((), ())), preferred_element_type=F32)


def _mm_tn(a, b):
    return lax.dot_general(a.astype(BF16), b.astype(BF16), (((0,), (0,)), ((), ())), preferred_element_type=F32)


def _mm_exact(a, b):
    return jnp.dot(a, b, precision=lax.Precision.HIGHEST, preferred_element_type=F32)


def _sigmoid(x):
    return 1.0 / (1.0 + jnp.exp(-x))


def _log_sigmoid(x):
    return -(jnp.maximum(-x, 0.0) + jnp.log1p(jnp.exp(-jnp.abs(x))))


def _layer_norm(z, g, b):
    mu = jnp.mean(z, axis=-1, keepdims=True)
    d = z - mu
    var = jnp.mean(d * d, axis=-1, keepdims=True)
    return d * lax.rsqrt(var + LN_EPS) * g + b


def _iota(shape, dim):
    return lax.broadcasted_iota(jnp.int32, shape, dim)


def _mods_kernel(c_ref, w_ref, b_ref, o_ref):
    c = c_ref[...]
    o_ref[...] = _mm(c * _sigmoid(c), w_ref[...]) + b_ref[...]


def _mods(c_all, w_mod, b_mod):
    rows, d = c_all.shape
    n = w_mod.shape[1]
    tn = n // 8
    out = pl.pallas_call(
        _mods_kernel,
        grid=(n // tn,),
        in_specs=[pl.BlockSpec((rows, d), lambda j: (0, 0)),
                  pl.BlockSpec((d, tn), lambda j: (0, j)),
                  pl.BlockSpec((1, tn), lambda j: (0, j))],
        out_specs=pl.BlockSpec((rows, tn), lambda j: (0, j)),
        out_shape=jax.ShapeDtypeStruct((rows, n), F32),
        compiler_params=_cparams("arbitrary"),
        name="mods",
    )(c_all, w_mod, b_mod[None])
    return out.reshape(rows, N_MOD, d)


def _mod_spec(d, ctx_row):
    if ctx_row is None:
        return pl.BlockSpec((1, N_MOD, d), lambda b, j: (b, 0, 0))
    return pl.BlockSpec((1, N_MOD, d), lambda b, j: (ctx_row, 0, 0))


def _ffn_kernel(x_ref, m_ref, wgu_ref, wd_ref, g_ref, b_ref, o_ref, *, i0, fc):
    hidden = wd_ref.shape[0]
    x = x_ref[0]
    shift, scale, gate = (m_ref[0, i0 + i:i0 + i + 1, :] for i in range(3))
    h = (x * (1.0 + scale) + shift).astype(BF16)
    acc = None
    for c in range(hidden // fc):
        g = jnp.dot(h, wgu_ref[:, c * fc:(c + 1) * fc], preferred_element_type=F32)
        u = jnp.dot(h, wgu_ref[:, hidden + c * fc:hidden + (c + 1) * fc], preferred_element_type=F32)
        a = (g * _sigmoid(g) * u).astype(BF16)
        dn = jnp.dot(a, wd_ref[c * fc:(c + 1) * fc, :], preferred_element_type=F32)
        acc = dn if acc is None else acc + dn
    z = DEEPNORM_ALPHA * x + gate * (0.5 * acc)
    o_ref[0] = _layer_norm(z, g_ref[...], b_ref[...])


def _ffn(x, mods, i0, w_gu, w_down, ln_g, ln_b, *, tm, ctx_row=None):
    bsz, t, d = x.shape
    hidden = w_down.shape[0]
    const = lambda b, j: (0, 0)
    return pl.pallas_call(
        functools.partial(_ffn_kernel, i0=i0, fc=256),
        grid=(bsz, t // tm),
        in_specs=[pl.BlockSpec((1, tm, d), lambda b, j: (b, j, 0)),
                  _mod_spec(d, ctx_row),
                  pl.BlockSpec((d, 2 * hidden), const, pipeline_mode=pl.Buffered(1)),
                  pl.BlockSpec((hidden, d), const, pipeline_mode=pl.Buffered(1)),
                  pl.BlockSpec((1, d), const),
                  pl.BlockSpec((1, d), const)],
        out_specs=pl.BlockSpec((1, tm, d), lambda b, j: (b, j, 0)),
        out_shape=jax.ShapeDtypeStruct(x.shape, F32),
        compiler_params=_cparams("parallel", "parallel"),
        name="ffn",
    )(x, mods, w_gu, w_down, ln_g[None], ln_b[None])


def _proj_kernel(x_ref, m_ref, w_ref, *rest, chunks, rope_chunk):
    if rope_chunk is None:
        (o_ref,) = rest
    else:
        cos_ref, sin_ref, o_ref = rest
    x = x_ref[0]
    h = (x * (1.0 + m_ref[0, 4:5, :]) + m_ref[0, 3:4, :]).astype(BF16)
    for ci, (lo, hi) in enumerate(chunks):
        u = jnp.dot(h, w_ref[:, lo:hi], preferred_element_type=F32)
        if ci == rope_chunk:
            n = hi - lo
            first = (_iota((1, n), 1) & 16) == 0
            partner = jnp.where(first, pltpu.roll(u, n - 16, axis=1), pltpu.roll(u, 16, axis=1))
            u = u * cos_ref[...] + partner * sin_ref[...]
        o_ref[0, :, lo:hi] = u


def _proj(x, mods, w_in, chunks, *, tm, ctx_row=None, rope=None):
    bsz, t, d = x.shape
    n = w_in.shape[1]
    const = lambda b, j: (0, 0)
    in_specs = [pl.BlockSpec((1, tm, d), lambda b, j: (b, j, 0)),
                _mod_spec(d, ctx_row),
                pl.BlockSpec((d, n), const, pipeline_mode=pl.Buffered(1))]
    args = [x, mods, w_in]
    rope_chunk = None
    if rope is not None:
        rope_chunk, cos_t, sin_t = rope
        width = cos_t.shape[1]
        in_specs += [pl.BlockSpec((tm, width), lambda b, j: (j, 0))] * 2
        args += [cos_t, sin_t]
    return pl.pallas_call(
        functools.partial(_proj_kernel, chunks=chunks, rope_chunk=rope_chunk),
        grid=(bsz, t // tm),
        in_specs=in_specs,
        out_specs=pl.BlockSpec((1, tm, n), lambda b, j: (b, j, 0)),
        out_shape=jax.ShapeDtypeStruct((bsz, t, n), F32),
        compiler_params=_cparams("parallel", "parallel"),
        name="proj",
    )(*args)


def _mixout_kernel(x_ref, m_ref, ya_ref, yb_ref, w_ref, g_ref, b_ref, o_ref):
    x = x_ref[0]
    half = ya_ref.shape[2]
    y = _mm(ya_ref[0], w_ref[0:half, :]) + _mm(yb_ref[0], w_ref[half:2 * half, :])
    z = DEEPNORM_ALPHA * x + m_ref[0, 5:6, :] * y
    o_ref[0] = _layer_norm(z, g_ref[...], b_ref[...])


def _mixout(x, mods, ya, yb, w_out, ln_g, ln_b, *, tm, ctx_row=None):
    bsz, t, d = x.shape
    half = ya.shape[2]
    const = lambda b, j: (0, 0)
    tok = lambda b, j: (b, j, 0)
    return pl.pallas_call(
        _mixout_kernel,
        grid=(bsz, t // tm),
        in_specs=[pl.BlockSpec((1, tm, d), tok), _mod_spec(d, ctx_row),
                  pl.BlockSpec((1, tm, half), tok), pl.BlockSpec((1, tm, half), tok),
                  pl.BlockSpec((2 * half, d), const, pipeline_mode=pl.Buffered(1)),
                  pl.BlockSpec((1, d), const), pl.BlockSpec((1, d), const)],
        out_specs=pl.BlockSpec((1, tm, d), tok),
        out_shape=jax.ShapeDtypeStruct(x.shape, F32),
        compiler_params=_cparams("parallel", "parallel"),
        name="mixout",
    )(x, mods, ya, yb, w_out, ln_g[None], ln_b[None])


def _rwkv_prep_kernel(u_ref, up_ref, un_ref, mu_ref, w0_ref, w2_ref, a0_ref, a2_ref, g2_ref, kk_ref, ka_ref, rk_ref,
                      eh_ref, r_o, v_o, kk_o, w_o, k_o, kka_o, bonus_o, g_o, *, tt):
    j = pl.program_id(1)
    nj = pl.num_programs(1)
    wdt = RWKV_WIDTH
    u = u_ref[0]
    prev_row = up_ref[0, SUBLANES - 1:SUBLANES, :] * (j > 0).astype(F32)
    next_row = un_ref[0, 0:1, :] * (j < nj - 1).astype(F32)
    row = _iota((tt, 1), 0)
    up = jnp.where(row == 0, prev_row, pltpu.roll(u, 1, axis=0))
    un = jnp.where(row == tt - 1, next_row, pltpu.roll(u, tt - 1, axis=0))
    us = u + mu_ref[...] * (0.5 * (up + un) - u)
    r, k, v = us[:, 0:wdt], us[:, wdt:2 * wdt], us[:, 2 * wdt:3 * wdt]
    wd = jnp.tanh(us[:, 3 * wdt:3 * wdt + 128])
    ad = us[:, 3 * wdt + 128:3 * wdt + 256]
    gd = us[:, 3 * wdt + 256:3 * wdt + 384]
    eh = eh_ref[...]
    kkn = k * kk_ref[...]
    kkn = kkn / jnp.maximum(jnp.sqrt(_mm_exact(kkn * kkn, eh)), 1e-12)
    bonus_o[0] = _mm_exact(r * k * rk_ref[...], eh) * v
    g_o[0] = _mm(_sigmoid(gd), g2_ref[...])
    r_o[0] = r
    v_o[0] = v
    kk_o[0] = kkn
    for d in range(2):
        wl = w0_ref[d:d + 1, :] + _mm(wd, w2_ref[d])
        w_o[d, 0] = jnp.exp(-jnp.exp(_log_sigmoid(wl) - 0.5))
        a = _sigmoid(a0_ref[d:d + 1, :] + _mm(ad, a2_ref[d]))
        k_o[d, 0] = k * (1.0 + (a - 1.0) * ka_ref[...])
        kka_o[d, 0] = kkn * a


def _rwkv_prep(u, p, *, tt):
    bsz, t, _ = u.shape
    wdt = RWKV_WIDTH
    nblk8 = t // SUBLANES
    per = tt // SUBLANES
    c2 = lambda b, j: (0, 0)
    c3 = lambda b, j: (0, 0, 0)
    tok = lambda b, j: (b, j, 0)
    tok2 = lambda b, j: (0, b, j, 0)
    one = jax.ShapeDtypeStruct((bsz, t, wdt), F32)
    two = jax.ShapeDtypeStruct((2, bsz, t, wdt), F32)
    return pl.pallas_call(
        functools.partial(_rwkv_prep_kernel, tt=tt),
        grid=(bsz, t // tt),
        in_specs=[pl.BlockSpec((1, tt, RWKV_IN), tok),
                  pl.BlockSpec((1, SUBLANES, RWKV_IN), lambda b, j: (b, jnp.maximum(j * per - 1, 0), 0)),
                  pl.BlockSpec((1, SUBLANES, RWKV_IN), lambda b, j: (b, jnp.minimum((j + 1) * per, nblk8 - 1), 0)),
                  pl.BlockSpec((1, RWKV_IN), c2),
                  pl.BlockSpec((2, wdt), c2), pl.BlockSpec((2, 128, wdt), c3),
                  pl.BlockSpec((2, wdt), c2), pl.BlockSpec((2, 128, wdt), c3),
                  pl.BlockSpec((128, wdt), c2),
                  pl.BlockSpec((1, wdt), c2), pl.BlockSpec((1, wdt), c2), pl.BlockSpec((1, wdt), c2),
                  pl.BlockSpec((wdt, wdt), c2)],
        out_specs=[pl.BlockSpec((1, tt, wdt), tok)] * 3 + [pl.BlockSpec((2, 1, tt, wdt), tok2)] * 3
                  + [pl.BlockSpec((1, tt, wdt), tok)] * 2,
        out_shape=[one, one, one, two, two, two, one, one],
        compiler_params=_cparams("parallel", "parallel"),
        name="rwkv_prep",
    )(u, u, u, p["mu"], p["w0"], p["w2"], p["a0"], p["a2"], p["g2"], p["k_k"], p["k_a"], p["r_k"], p["eh"])


def _rwkv_scan_kernel(r_ref, v_ref, kk_ref, w_ref, k_ref, kka_ref, s0_ref, y_ref, sT_ref, s_ref, *, tb, k_unroll):
    d = pl.program_id(0)
    j = pl.program_id(1)
    nk = RWKV_HEAD_DIM
    nvb = RWKV_HEAD_DIM // SUBLANES
    lanes = s_ref.shape[1]

    @pl.when(j == 0)
    def _():
        s_ref[...] = s0_ref[...]

    def bcast(ref, t, k):
        return jnp.broadcast_to(ref[t, pl.ds(k, 1), :], (SUBLANES, lanes))

    def srow(k, vb):
        return pl.ds(pl.multiple_of(k * nk + vb * SUBLANES, SUBLANES), SUBLANES)

    def step(i, carry):
        t = jnp.where(d == 0, i, tb - 1 - i)

        def reduce_body(kc, acc):
            acc = list(acc)
            for kq in range(k_unroll):
                k = kc * k_unroll + kq
                kkb = bcast(kk_ref, t, k)
                wrb = bcast(w_ref, t, k) * bcast(r_ref, t, k)
                for vb in range(nvb):
                    s = s_ref[srow(k, vb), :]
                    acc[vb] = acc[vb] + s * kkb
                    acc[nvb + vb] = acc[nvb + vb] + s * wrb
            return tuple(acc)

        zero = jnp.zeros((SUBLANES, lanes), F32)
        acc = lax.fori_loop(0, nk // k_unroll, reduce_body, (zero,) * (2 * nvb))
        skk = acc[:nvb]
        rt = r_ref[t]
        c1 = jnp.sum(kka_ref[t] * rt, axis=0, keepdims=True)
        c2 = jnp.sum(k_ref[t] * rt, axis=0, keepdims=True)
        vt = [v_ref[t, vb * SUBLANES:(vb + 1) * SUBLANES, :] for vb in range(nvb)]
        for vb in range(nvb):
            y_ref[t, vb * SUBLANES:(vb + 1) * SUBLANES, :] = acc[nvb + vb] - skk[vb] * c1 + vt[vb] * c2

        def update_body(kc, c):
            for kq in range(k_unroll):
                k = kc * k_unroll + kq
                wb = bcast(w_ref, t, k)
                ab = bcast(kka_ref, t, k)
                kb = bcast(k_ref, t, k)
                for vb in range(nvb):
                    idx = srow(k, vb)
                    s_ref[idx, :] = s_ref[idx, :] * wb - skk[vb] * ab + vt[vb] * kb
            return c

        lax.fori_loop(0, nk // k_unroll, update_body, 0)
        return carry

    lax.fori_loop(0, tb, step, 0)

    @pl.when(j == pl.num_programs(1) - 1)
    def _():
        sT_ref[...] = s_ref[...]


def _rwkv_scan(r, v, kk, w, k, kka, s0, *, tb):
    t, nk, lanes = r.shape
    nblk = t // tb
    shared = pl.BlockSpec((tb, nk, lanes), lambda d, j: (j + d * (nblk - 1 - 2 * j), 0, 0))
    perdir = pl.BlockSpec((None, tb, nk, lanes), lambda d, j: (d, j + d * (nblk - 1 - 2 * j), 0, 0))
    state = pl.BlockSpec((None, nk * nk, lanes), lambda d, j: (d, 0, 0))
    return pl.pallas_call(
        functools.partial(_rwkv_scan_kernel, tb=tb, k_unroll=4),
        grid=(2, nblk),
        in_specs=[shared, shared, shared, perdir, perdir, perdir, state],
        out_specs=[perdir, state],
        out_shape=[jax.ShapeDtypeStruct((2, t, nk, lanes), F32), jax.ShapeDtypeStruct((2, nk * nk, lanes), F32)],
        scratch_shapes=[pltpu.VMEM((nk * nk, lanes), F32)],
        compiler_params=_cparams("arbitrary", "arbitrary"),
        name="rwkv_scan",
    )(r, v, kk, w, k, kka, s0)


def _rwkv_out_kernel(yf_ref, yb_ref, bonus_ref, g_ref, gng_ref, gnb_ref, em_ref, o_ref):
    y = yf_ref[0, 0] + yb_ref[0, 0]
    em = em_ref[...]
    dlt = y - _mm_exact(y, em)
    var = _mm_exact(dlt * dlt, em)
    yn = dlt * lax.rsqrt(var + RWKV_GN_EPS) * gng_ref[...] + gnb_ref[...]
    o_ref[0] = (yn + bonus_ref[0]) * g_ref[0]


def _rwkv_out(y2, bonus, g, gn_g, gn_b, em, *, tt):
    _, bsz, t, wdt = y2.shape
    c2 = lambda b, j: (0, 0)
    tok = lambda b, j: (b, j, 0)
    return pl.pallas_call(
        _rwkv_out_kernel,
        grid=(bsz, t // tt),
        in_specs=[pl.BlockSpec((1, 1, tt, wdt), lambda b, j: (0, b, j, 0)),
                  pl.BlockSpec((1, 1, tt, wdt), lambda b, j: (1, b, j, 0)),
                  pl.BlockSpec((1, tt, wdt), tok), pl.BlockSpec((1, tt, wdt), tok),
                  pl.BlockSpec((1, wdt), c2), pl.BlockSpec((1, wdt), c2), pl.BlockSpec((wdt, wdt), c2)],
        out_specs=pl.BlockSpec((1, tt, wdt), tok),
        out_shape=jax.ShapeDtypeStruct((bsz, t, wdt), F32),
        compiler_params=_cparams("parallel", "parallel"),
        name="rwkv_out",
    )(y2, y2, bonus, g, gn_g, gn_b, em)


def _to_time_major(a):
    *lead, bsz, t, _ = a.shape
    nl = len(lead)
    a = a.reshape(*lead, bsz, t, RWKV_HEADS, RWKV_HEAD_DIM)
    a = a.transpose(*range(nl), nl + 1, nl + 3, nl, nl + 2)
    return a.reshape(*lead, t, RWKV_HEAD_DIM, bsz * RWKV_HEADS)


def _from_time_major(a, bsz):
    two, t, n, _ = a.shape
    a = a.reshape(two, t, n, bsz, RWKV_HEADS).transpose(0, 3, 1, 4, 2)
    return a.reshape(two, bsz, t, RWKV_HEADS * n)


def _na_kernel(q_ref, k_ref, v_ref, qc_ref, kc_ref, vc_ref, bias_ref, y_ref, yc_ref, kb_s, vb_s, kcb_s, vcb_s, *, rows):
    scale = NA_HEAD_DIM ** -0.5
    wq = GRID_W
    band = NA_WIN_R * GRID_W
    kb_s[...] = k_ref[0].astype(BF16)
    vb_s[...] = v_ref[0].astype(BF16)
    kcb_s[...] = kc_ref[0].astype(BF16)
    vcb_s[...] = vc_ref[0].astype(BF16)
    lane = _iota((1, LANES), 1)
    head_mask = [lane < NA_HEAD_DIM, lane >= NA_HEAD_DIM]

    def attend(qs, parts):
        out = jnp.zeros(qs.shape, F32)
        for h in range(2):
            qh = jnp.where(head_mask[h], qs, 0.0)
            scores = []
            for keys, _, bias in parts:
                s = _mm_nt(qh, keys)
                scores.append(s if bias is None else s + bias[h])
            m = functools.reduce(jnp.maximum, [jnp.max(s, axis=-1, keepdims=True) for s in scores])
            ps = [jnp.exp(s - m) for s in scores]
            l = functools.reduce(lambda a, b: a + b, [jnp.sum(p, axis=-1, keepdims=True) for p in ps])
            o = functools.reduce(lambda a, b: a + b, [_mm(p, vals) for p, (_, vals, _) in zip(ps, parts)])
            out = jnp.where(head_mask[h], o / l, out)
        return out

    def one_row(r, c):
        rs = jnp.clip(r - NA_WIN_R // 2, 0, rows - NA_WIN_R)
        var = jnp.where(r < NA_WIN_R // 2, r, jnp.where(r > rows - NA_WIN_R // 2, r - (rows - NA_WIN_R), NA_WIN_R // 2))
        q0 = pl.multiple_of(r * wq, wq)
        k0 = pl.multiple_of(rs * wq, wq)
        qs = q_ref[0, pl.ds(q0, wq), :] * scale
        bias = [bias_ref[h, var] for h in range(2)]
        y_ref[0, pl.ds(q0, wq), :] = attend(qs, [(kb_s[pl.ds(k0, band), :], vb_s[pl.ds(k0, band), :], bias),
                                                 (kcb_s[...], vcb_s[...], None)])
        return c

    lax.fori_loop(0, rows, one_row, 0)
    yc_ref[0] = attend(qc_ref[0] * scale, [(kcb_s[...], vcb_s[...], None)])


def _na_bias_table(rpb, rows):
    cols = jnp.arange(GRID_W)
    col_start = jnp.clip(cols - NA_WIN_C // 2, 0, GRID_W - NA_WIN_C)
    col_in = (cols[None, :] >= col_start[:, None]) & (cols[None, :] < col_start[:, None] + NA_WIN_C)
    dc_idx = jnp.clip(cols[None, :] - cols[:, None] + NA_WIN_C - 1, 0, 2 * NA_WIN_C - 2)
    rpb_cols = jnp.where(col_in[None, None], rpb[:, :, dc_idx], -jnp.inf)
    half = NA_WIN_R // 2
    rep_rows = list(range(half)) + [half] + list(range(rows - half + 1, rows))
    tiles = []
    for r in rep_rows:
        rs = min(max(r - half, 0), rows - NA_WIN_R)
        dr_idx = rs + jnp.arange(NA_WIN_R) - r + NA_WIN_R - 1
        t = rpb_cols[:, dr_idx]
        tiles.append(t.transpose(0, 2, 1, 3).reshape(NA_HEADS, GRID_W, NA_WIN_R * GRID_W))
    return jnp.stack(tiles, axis=1)


def _na(u, uc, bias_tab, col0):
    bsz, s, _ = u.shape
    lc = uc.shape[1]
    rows = s // GRID_W
    assert rows >= 2 * NA_WIN_R and len(range(NA_WIN_R)) == bias_tab.shape[1]
    qb, kb, vb = col0 // LANES, col0 // LANES + 4, col0 // LANES + 8
    band = NA_WIN_R * GRID_W
    lat = lambda cb: pl.BlockSpec((1, s, LANES), lambda b, p: (b, 0, cb + p))
    cx = lambda cb: pl.BlockSpec((1, lc, LANES), lambda b, p: (b, 0, cb + p))
    return pl.pallas_call(
        functools.partial(_na_kernel, rows=rows),
        grid=(bsz, NA_HEADS // 2),
        in_specs=[lat(qb), lat(kb), lat(vb), cx(qb), cx(kb), cx(vb),
                  pl.BlockSpec((2, NA_WIN_R, GRID_W, band), lambda b, p: (p, 0, 0, 0))],
        out_specs=[pl.BlockSpec((1, s, LANES), lambda b, p: (b, 0, p)),
                   pl.BlockSpec((1, lc, LANES), lambda b, p: (b, 0, p))],
        out_shape=[jax.ShapeDtypeStruct((bsz, s, NA_HEADS * NA_HEAD_DIM), F32),
                   jax.ShapeDtypeStruct((bsz, lc, NA_HEADS * NA_HEAD_DIM), F32)],
        scratch_shapes=[pltpu.VMEM((s, LANES), BF16), pltpu.VMEM((s, LANES), BF16),
                        pltpu.VMEM((lc, LANES), BF16), pltpu.VMEM((lc, LANES), BF16)],
        compiler_params=_cparams("parallel", "arbitrary"),
        name="na",
    )(u, u, u, uc, uc, uc, bias_tab)


def _gla_kernel(q_ref, k_ref, v_ref, r_ref, gd_ref, kc_ref, vc_ref, gdc_ref, g2_ref, gb_ref, ng_ref, y_ref,
                bc_s, qt_s, kt_s, dec_s, ktc_s, decc_s, st_s, of_s):
    s_len = q_ref.shape[1]
    l_len = kc_ref.shape[1]
    cs = GLA_SUB
    scale = GLA_DK ** -0.5
    ri = _iota((LANES, LANES), 0)
    ci = _iota((LANES, LANES), 1)
    sub_shift = cs.bit_length() - 1
    same_chunk = (ri >> sub_shift) == (ci >> sub_shift)
    lane = _iota((1, LANES), 1)
    head_lane = [lane < GLA_DK, lane >= GLA_DK]
    st_mask = (_iota((2 * GLA_DV, LANES), 0) >= GLA_DV) == (_iota((2 * GLA_DV, LANES), 1) >= GLA_DK)
    trow = _iota((cs, 1), 0)

    for d in range(2):
        fwd = d == 0
        tri = (same_chunk & ((ci <= ri) if fwd else (ci >= ri))).astype(F32)
        ones_blk = same_chunk.astype(F32)

        def gates(gd, d=d, tri=tri, ones_blk=ones_blk):
            g = _log_sigmoid(_mm(gd, g2_ref[d, 0]) + gb_ref[d, 0]) / GLA_GATE_NORM
            return _mm_exact(tri, g), _mm_exact(ones_blk, g)

        def prep_lat(i, c):
            rows = pl.ds(pl.multiple_of(i * LANES, LANES), LANES)
            bc, tot = gates(gd_ref[0, rows, :])
            bc_s[rows, :] = bc
            qt_s[rows, :] = q_ref[0, rows, :] * scale * jnp.exp(bc)
            kt_s[rows, :] = k_ref[0, rows, :] * jnp.exp(tot - bc)
            dec_s[rows, :] = jnp.exp(tot)
            return c

        def prep_ctx(i, c):
            rows = pl.ds(pl.multiple_of(i * LANES, LANES), LANES)
            bc, tot = gates(gdc_ref[0, rows, :])
            ktc_s[rows, :] = kc_ref[0, rows, :] * jnp.exp(tot - bc)
            decc_s[rows, :] = jnp.exp(tot)
            return c

        lax.fori_loop(0, s_len // LANES, prep_lat, 0)
        lax.fori_loop(0, l_len // LANES, prep_ctx, 0)
        st_s[...] = jnp.zeros(st_s.shape, F32)

        def advance(kt, vv, dec):
            st_s[...] = st_s[...] * dec + jnp.where(st_mask, _mm_tn(vv, kt), 0.0)

        def ctx_chunk(i, c, fwd=fwd):
            ch = i if fwd else l_len // cs - 1 - i
            rows = pl.ds(pl.multiple_of(ch * cs, cs), cs)
            advance(ktc_s[rows, :], vc_ref[0, rows, :], decc_s[rows, :][0:1, :])
            return c

        def lat_chunk(i, c, fwd=fwd):
            ch = i if fwd else s_len // cs - 1 - i
            rows = pl.ds(pl.multiple_of(ch * cs, cs), cs)
            vv = v_ref[0, rows, :]
            o = _mm_nt(qt_s[rows, :], st_s[...])
            bc = bc_s[rows, :]
            qs = q_ref[0, rows, :] * scale
            kk = k_ref[0, rows, :]
            o0, o1 = o[:, 0:GLA_DV], o[:, GLA_DV:2 * GLA_DV]
            for s in range(cs):
                keep = (trow >= s) if fwd else (trow <= s)
                e = jnp.exp(jnp.where(keep, bc - bc[s:s + 1, :], -jnp.inf))
                term = qs * kk[s:s + 1, :] * e
                a0 = jnp.sum(jnp.where(head_lane[0], term, 0.0), axis=-1, keepdims=True)
                a1 = jnp.sum(jnp.where(head_lane[1], term, 0.0), axis=-1, keepdims=True)
                o0 = o0 + a0 * vv[s:s + 1, 0:GLA_DV]
                o1 = o1 + a1 * vv[s:s + 1, GLA_DV:2 * GLA_DV]
            advance(kt_s[rows, :], vv, dec_s[rows, :][0:1, :])
            if fwd:
                of_s[rows, 0:GLA_DV] = o0
                of_s[rows, GLA_DV:2 * GLA_DV] = o1
            else:
                rr = r_ref[0, rows, :]
                for h, oh in enumerate((o0, o1)):
                    cols = slice(h * GLA_DV, (h + 1) * GLA_DV)
                    ot = of_s[rows, cols] + oh
                    on = ot * lax.rsqrt(jnp.mean(ot * ot, axis=-1, keepdims=True) + RMS_EPS) * ng_ref[...]
                    rh = rr[:, cols]
                    y_ref[0, rows, cols] = on * (rh * _sigmoid(rh))
            return c

        lax.fori_loop(0, l_len // cs, ctx_chunk, 0)
        lax.fori_loop(0, s_len // cs, lat_chunk, 0)


def _gla(u, uc, g2p, gbp, norm_g, gd_block):
    bsz, s, _ = u.shape
    lc = uc.shape[1]
    pairs = GLA_HEADS // 2
    dv2 = 2 * GLA_DV
    lat = lambda width, blk: pl.BlockSpec((1, s, width), lambda b, p: (b, 0, blk(p)))
    cx = lambda width, blk: pl.BlockSpec((1, lc, width), lambda b, p: (b, 0, blk(p)))
    q_blk = lambda p: p
    k_blk = lambda p: pairs + p
    v_blk = lambda p: 2 * pairs * LANES // dv2 + p
    r_blk = lambda p: (2 * pairs * LANES + GLA_HEADS * GLA_DV) // dv2 + p
    gd_blk = lambda p: gd_block
    return pl.pallas_call(
        _gla_kernel,
        grid=(bsz, pairs),
        in_specs=[lat(LANES, q_blk), lat(LANES, k_blk), lat(dv2, v_blk), lat(dv2, r_blk), lat(LANES, gd_blk),
                  cx(LANES, k_blk), cx(dv2, v_blk), cx(LANES, gd_blk),
                  pl.BlockSpec((2, 1, LANES, LANES), lambda b, p: (0, p, 0, 0)),
                  pl.BlockSpec((2, 1, 1, LANES), lambda b, p: (0, p, 0, 0)),
                  pl.BlockSpec((1, GLA_DV), lambda b, p: (0, 0))],
        out_specs=pl.BlockSpec((1, s, dv2), lambda b, p: (b, 0, p)),
        out_shape=jax.ShapeDtypeStruct((bsz, s, GLA_HEADS * GLA_DV), F32),
        scratch_shapes=[pltpu.VMEM((s, LANES), F32)] * 4 + [pltpu.VMEM((lc, LANES), F32)] * 2
                       + [pltpu.VMEM((dv2, LANES), F32), pltpu.VMEM((s, dv2), F32)],
        compiler_params=_cparams("parallel", "arbitrary"),
        name="gla",
    )(u, u, u, u, u, uc, uc, uc, g2p, gbp, norm_g)


def _diff_kernel(q_ref, k_ref, v_ref, kc_ref, vc_ref, lam_ref, ng_ref, y_ref, *, lambda_init):
    scale = DIFF_DH ** -0.5
    lp = lam_ref[...]
    lam = (jnp.exp(jnp.sum(lp[0:1] * lp[1:2], axis=-1, keepdims=True))
           - jnp.exp(jnp.sum(lp[2:3] * lp[3:4], axis=-1, keepdims=True)) + lambda_init)
    qs = q_ref[0] * scale
    kb, vb = k_ref[0].astype(BF16), v_ref[0].astype(BF16)
    kcb, vcb = kc_ref[0].astype(BF16), vc_ref[0].astype(BF16)
    lane = _iota((1, LANES), 1)
    outs = []
    for m in range(2):
        qm = jnp.where((lane >= m * DIFF_DH) & (lane < (m + 1) * DIFF_DH), qs, 0.0)
        s = _mm_nt(qm, kb)
        sc = _mm_nt(qm, kcb)
        mx = jnp.maximum(jnp.max(s, axis=-1, keepdims=True), jnp.max(sc, axis=-1, keepdims=True))
        p, pc = jnp.exp(s - mx), jnp.exp(sc - mx)
        l = jnp.sum(p, axis=-1, keepdims=True) + jnp.sum(pc, axis=-1, keepdims=True)
        outs.append((_mm(p, vb) + _mm(pc, vcb)) / l)
    o = outs[0] - lam * outs[1]
    on = o * lax.rsqrt(jnp.mean(o * o, axis=-1, keepdims=True) + RMS_EPS) * ng_ref[...]
    y_ref[0] = on * (1.0 - lambda_init)


def _diff(u, uc, lam_params, norm_g, col0, lambda_init, *, tq):
    bsz, s, _ = u.shape
    lc = uc.shape[1]
    qb = col0 // LANES
    kb, vb = qb + DIFF_HEADS, qb + 2 * DIFF_HEADS
    full = lambda arr_len, cb: pl.BlockSpec((1, arr_len, LANES), lambda b, h, j: (b, 0, cb + h))
    return pl.pallas_call(
        functools.partial(_diff_kernel, lambda_init=lambda_init),
        grid=(bsz, DIFF_HEADS, s // tq),
        in_specs=[pl.BlockSpec((1, tq, LANES), lambda b, h, j: (b, j, qb + h)),
                  full(s, kb), full(s, vb), full(lc, kb), full(lc, vb),
                  pl.BlockSpec((4, DIFF_DH), lambda b, h, j: (0, 0)),
                  pl.BlockSpec((1, DIFF_DV), lambda b, h, j: (0, 0))],
        out_specs=pl.BlockSpec((1, tq, LANES), lambda b, h, j: (b, j, h)),
        out_shape=jax.ShapeDtypeStruct((bsz, s, DIFF_HEADS * DIFF_DV), F32),
        compiler_params=_cparams("parallel", "parallel", "arbitrary"),
        name="diff_attn",
    )(u, u, u, uc, uc, lam_params, norm_g)


def _head_selector(width, head_dim, value):
    idx = jnp.arange(width) // head_dim
    return jnp.where(idx[:, None] == idx[None, :], value, 0.0).astype(F32)


def _pad_rows(w, lo, total):
    return jnp.zeros((total, w.shape[1]), w.dtype).at[lo:lo + w.shape[0]].set(w)


def _rope_tables(s, reps):
    pos = jnp.arange(s)
    n = DIFF_DH // 4
    freqs = ROPE_BASE ** (-jnp.arange(n, dtype=F32) / n)
    ang_r = (pos // GRID_W).astype(F32)[:, None] * freqs[None, :]
    ang_c = (pos % GRID_W).astype(F32)[:, None] * freqs[None, :]
    cos = jnp.concatenate([jnp.cos(ang_r)] * 2 + [jnp.cos(ang_c)] * 2, axis=-1)
    sin = jnp.concatenate([-jnp.sin(ang_r), jnp.sin(ang_r), -jnp.sin(ang_c), jnp.sin(ang_c)], axis=-1)
    return jnp.tile(cos, (1, reps)), jnp.tile(sin, (1, reps))


def _rwkv_na_mixer(xl, xc, mods, ctx_row, w_in, mu, w0, w2, a0, a2, g2, k_k, k_a, r_k, gn_g, gn_b, rpb):
    bsz, s, _ = xl.shape
    lc = xc.shape[1]
    n = w_in.shape[1]
    chunks = tuple((lo, min(lo + 512, n)) for lo in range(0, n, 512))
    u = _proj(xl, mods, w_in, chunks, tm=512)
    uc = _proj(xc, mods, w_in, chunks, tm=lc, ctx_row=ctx_row)

    eh = _head_selector(RWKV_WIDTH, RWKV_HEAD_DIM, 1.0)
    em = _head_selector(RWKV_WIDTH, RWKV_HEAD_DIM, 1.0 / RWKV_HEAD_DIM)
    p = dict(mu=mu[None], w0=w0, a0=a0, g2=g2.astype(BF16), k_k=k_k[None], k_a=k_a[None], r_k=r_k.reshape(1, -1), eh=eh,
             w2=jnp.stack([_pad_rows(w2[0], 0, 128), _pad_rows(w2[1], 64, 128)]).astype(BF16),
             a2=jnp.stack([_pad_rows(a2[0], 0, 128), _pad_rows(a2[1], 64, 128)]).astype(BF16))
    lanes = bsz * RWKV_HEADS
    state = jnp.zeros((2, RWKV_HEAD_DIM * RWKV_HEAD_DIM, lanes), F32)
    ys = []
    for uu, tt, tb in ((uc, lc, 16), (u, 256, 16)):
        r, v, kk, w, kd, kka, bonus, g = _rwkv_prep(uu, p, tt=tt)
        y2, state = _rwkv_scan(_to_time_major(r), _to_time_major(v), _to_time_major(kk), _to_time_major(w),
                               _to_time_major(kd), _to_time_major(kka), state, tb=tb)
        ys.append(_rwkv_out(_from_time_major(y2, bsz), bonus, g, gn_g[None], gn_b[None], em, tt=tt))
    yc_a, y_a = ys
    y_b, yc_b = _na(u, uc, _na_bias_table(rpb, s // GRID_W), RWKV_IN)
    return (y_a, y_b), (yc_a, yc_b)


def _gla_diff_mixer(xl, xc, mods, ctx_row, w_in, gla_g2, gla_gb, gla_norm_g, diff_lambda, diff_norm_g, lambda_init):
    bsz, s, d = xl.shape
    lc = xc.shape[1]
    n_gla = 2 * GLA_HEADS * GLA_DK + 2 * GLA_HEADS * GLA_DV
    n_gate = 2 * GLA_GATE_LORA
    n_diff = 2 * DIFF_HEADS * 2 * DIFF_DH + DIFF_HEADS * DIFF_DV
    w_perm = jnp.concatenate([w_in[:, :n_gla], w_in[:, n_gla + n_gate:n_gla + n_gate + n_diff],
                              w_in[:, n_gla:n_gla + n_gate], jnp.zeros((d, LANES - n_gate), w_in.dtype)], axis=1).astype(BF16)
    rope_w = 2 * DIFF_HEADS * 2 * DIFF_DH
    chunks = ((0, 512), (512, 1024), (1024, n_gla), (n_gla, n_gla + rope_w),
              (n_gla + rope_w, n_gla + n_diff), (n_gla + n_diff, n_gla + n_diff + LANES))
    cos_t, sin_t = _rope_tables(s, rope_w // DIFF_DH)
    u = _proj(xl, mods, w_perm, chunks, tm=512, rope=(3, cos_t, sin_t))
    uc = _proj(xc, mods, w_perm, chunks, tm=lc, ctx_row=ctx_row)

    pairs = GLA_HEADS // 2
    g2p = jnp.stack([jnp.stack([_pad_rows(gla_g2[dd][:, p * LANES:(p + 1) * LANES], dd * GLA_GATE_LORA, LANES)
                                for p in range(pairs)]) for dd in range(2)]).astype(BF16)
    gbp = gla_gb.reshape(2, pairs, 1, LANES)
    y_c = _gla(u, uc, g2p, gbp, gla_norm_g[None], (n_gla + n_diff) // LANES)
    y_d = _diff(u, uc, diff_lambda, diff_norm_g[None], n_gla, lambda_init, tq=512)
    return y_c, y_d


def kernel(x, c, ctx, c_ctx, w_mod_0, b_mod_0, ln_g_0, ln_b_0, ffn_gu_0, ffn_down_0, w_in_0, w_out_0, rwkv_mu_0, rwkv_w0_0, rwkv_w2_0, rwkv_a0_0, rwkv_a2_0, rwkv_g2_0, rwkv_k_k_0, rwkv_k_a_0, rwkv_r_k_0, rwkv_gn_g_0, rwkv_gn_b_0, na_rpb_0, w_mod_1, b_mod_1, ln_g_1, ln_b_1, ffn_gu_1, ffn_down_1, w_in_1, w_out_1, gla_g2_1, gla_gb_1, gla_norm_g_1, diff_lambda_1, diff_norm_g_1):
    bsz, s, d = x.shape
    lc = ctx.shape[1]
    ctx_row = bsz
    rows = -(-(bsz + 1) // SUBLANES) * SUBLANES
    c_all = jnp.concatenate([c, c_ctx[None], jnp.zeros((rows - bsz - 1, d), F32)], axis=0)
    tm = 512

    mods = _mods(c_all, w_mod_0, b_mod_0)
    gu, dn = ffn_gu_0.astype(BF16), ffn_down_0.astype(BF16)
    xl = _ffn(x, mods, 0, gu[0], dn[0], ln_g_0[0], ln_b_0[0], tm=tm)
    xc = _ffn(ctx, mods, 0, gu[0], dn[0], ln_g_0[0], ln_b_0[0], tm=lc, ctx_row=ctx_row)
    (y_a, y_b), (yc_a, yc_b) = _rwkv_na_mixer(xl, xc, mods, ctx_row, w_in_0.astype(BF16), rwkv_mu_0, rwkv_w0_0, rwkv_w2_0,
                                              rwkv_a0_0, rwkv_a2_0, rwkv_g2_0, rwkv_k_k_0, rwkv_k_a_0, rwkv_r_k_0,
                                              rwkv_gn_g_0, rwkv_gn_b_0, na_rpb_0)
    wo = w_out_0.astype(BF16)
    xl = _mixout(xl, mods, y_a, y_b, wo, ln_g_0[1], ln_b_0[1], tm=tm)
    xc = _mixout(xc, mods, yc_a, yc_b, wo, ln_g_0[1], ln_b_0[1], tm=lc, ctx_row=ctx_row)
    xl = _ffn(xl, mods, 6, gu[1], dn[1], ln_g_0[2], ln_b_0[2], tm=tm)
    xc = _ffn(xc, mods, 6, gu[1], dn[1], ln_g_0[2], ln_b_0[2], tm=lc, ctx_row=ctx_row)

    mods = _mods(c_all, w_mod_1, b_mod_1)
    gu, dn = ffn_gu_1.astype(BF16), ffn_down_1.astype(BF16)
    xl = _ffn(xl, mods, 0, gu[0], dn[0], ln_g_1[0], ln_b_1[0], tm=tm)
    xc = _ffn(xc, mods, 0, gu[0], dn[0], ln_g_1[0], ln_b_1[0], tm=lc, ctx_row=ctx_row)
    y_c, y_d = _gla_diff_mixer(xl, xc, mods, ctx_row, w_in_1, gla_g2_1, gla_gb_1, gla_norm_g_1, diff_lambda_1, diff_norm_g_1,
                               0.8 - 0.6 * math.exp(-0.3 * 1))
    xl = _mixout(xl, mods, y_c, y_d, w_out_1.astype(BF16), ln_g_1[1], ln_b_1[1], tm=tm)
    xl = _ffn(xl, mods, 6, gu[1], dn[1], ln_g_1[2], ln_b_1[2], tm=tm)
    return xl
```

```python
import functools
import math

import jax
import jax.numpy as jnp
from jax import lax
from jax.experimental import pallas as pl
from jax.experimental.pallas import tpu as pltpu

F32 = jnp.float32
BF16 = jnp.bfloat16

GRID_W = 64
N_MOD = 9
DEPTH = 2
LN_EPS = 1e-5
RMS_EPS = 1e-6
DEEPNORM_ALPHA = (2 * DEPTH) ** 0.25
RWKV_HEADS = 8
RWKV_HEAD_DIM = 64
RWKV_WIDTH = RWKV_HEADS * RWKV_HEAD_DIM
RWKV_IN = 3 * RWKV_WIDTH + 4 * 64 + 128
RWKV_GN_EPS = 64e-5
NA_HEADS = 8
NA_HEAD_DIM = 64
NA_WIN_R = 8
NA_WIN_C = 16
NA_ROW_GROUP = 2
GLA_HEADS = 4
GLA_DK = 64
GLA_DV = 128
GLA_GATE_LORA = 16
GLA_GATE_NORM = 16.0
GLA_SUB = 16
GLA_GROUP = 4
DIFF_HEADS = 4
DIFF_DH = 64
DIFF_DV = 128
ROPE_BASE = 10000.0

LANES = 128
SUBLANES = 8
V7X_VMEM_BYTES = 64 * 1024 * 1024
VMEM_LIMIT = (V7X_VMEM_BYTES * 7) // 8


def _cparams(*sem):
    return pltpu.CompilerParams(dimension_semantics=sem, vmem_limit_bytes=VMEM_LIMIT)


def _mm(a, b):
    return jnp.dot(a.astype(BF16), b.astype(BF16), preferred_element_type=F32)


def _mm_nt(a, b):
    return lax.dot_general(a.astype(BF16), b.astype(BF16), (((1,), (1,)), ((), ())), preferred_element_type=F32)


def _mm_tn(a, b):
    return lax.dot_general(a.astype(BF16), b.astype(BF16), (((0,), (0,)), ((), ())), preferred_element_type=F32)


def _split3(x):
    hi = x.astype(BF16)
    r1 = x - hi.astype(F32)
    mid = r1.astype(BF16)
    lo = (r1 - mid.astype(F32)).astype(BF16)
    return hi, mid, lo


def _sel_right(x, sel):
    m = x.shape[0]
    y = jnp.dot(jnp.concatenate(_split3(x), axis=0), sel, preferred_element_type=F32)
    return y[0:m] + y[m:2 * m] + y[2 * m:3 * m]


def _sel_left(sel, x):
    n = x.shape[1]
    y = jnp.dot(sel, jnp.concatenate(_split3(x), axis=1), preferred_element_type=F32)
    return y[:, 0:n] + y[:, n:2 * n] + y[:, 2 * n:3 * n]


def _sigmoid(x):
    return 1.0 / (1.0 + jnp.exp(-x))


def _log_sigmoid(x):
    return -(jnp.maximum(-x, 0.0) + jnp.log1p(jnp.exp(-jnp.abs(x))))


def _layer_norm(z, g, b):
    mu = jnp.mean(z, axis=-1, keepdims=True)
    d = z - mu
    var = jnp.mean(d * d, axis=-1, keepdims=True)
    return d * lax.rsqrt(var + LN_EPS) * g + b


def _iota(shape, dim):
    return lax.broadcasted_iota(jnp.int32, shape, dim)


def _mods_kernel(c_ref, w_ref, b_ref, o_ref):
    c = c_ref[...]
    o_ref[...] = _mm(c * _sigmoid(c), w_ref[...]) + b_ref[...]


def _mods(c_all, w_mod, b_mod):
    rows, d = c_all.shape
    n = w_mod.shape[1]
    tn = n // 8
    out = pl.pallas_call(
        _mods_kernel,
        grid=(n // tn,),
        in_specs=[pl.BlockSpec((rows, d), lambda j: (0, 0)),
                  pl.BlockSpec((d, tn), lambda j: (0, j)),
                  pl.BlockSpec((1, tn), lambda j: (0, j))],
        out_specs=pl.BlockSpec((rows, tn), lambda j: (0, j)),
        out_shape=jax.ShapeDtypeStruct((rows, n), F32),
        compiler_params=_cparams("arbitrary"),
        name="mods",
    )(c_all, w_mod, b_mod[None])
    return out.reshape(rows, N_MOD, d)


def _mod_spec(d, ctx_row):
    if ctx_row is None:
        return pl.BlockSpec((1, N_MOD, d), lambda b, j: (b, 0, 0))
    return pl.BlockSpec((1, N_MOD, d), lambda b, j: (ctx_row, 0, 0))


def _ffn_kernel(x_ref, m_ref, wgu_ref, wd_ref, g_ref, b_ref, o_ref, *, i0, fc):
    hidden = wd_ref.shape[0]
    x = x_ref[0]
    shift, scale, gate = (m_ref[0, i0 + i:i0 + i + 1, :] for i in range(3))
    h = (x * (1.0 + scale) + shift).astype(BF16)
    acc = None
    for c in range(hidden // fc):
        g = jnp.dot(h, wgu_ref[:, c * fc:(c + 1) * fc], preferred_element_type=F32)
        u = jnp.dot(h, wgu_ref[:, hidden + c * fc:hidden + (c + 1) * fc], preferred_element_type=F32)
        a = (g * _sigmoid(g) * u).astype(BF16)
        dn = jnp.dot(a, wd_ref[c * fc:(c + 1) * fc, :], preferred_element_type=F32)
        acc = dn if acc is None else acc + dn
    z = DEEPNORM_ALPHA * x + gate * (0.5 * acc)
    o_ref[0] = _layer_norm(z, g_ref[...], b_ref[...])


def _ffn(x, mods, i0, w_gu, w_down, ln_g, ln_b, *, tm, ctx_row=None):
    bsz, t, d = x.shape
    hidden = w_down.shape[0]
    const = lambda b, j: (0, 0)
    return pl.pallas_call(
        functools.partial(_ffn_kernel, i0=i0, fc=256),
        grid=(bsz, t // tm),
        in_specs=[pl.BlockSpec((1, tm, d), lambda b, j: (b, j, 0)),
                  _mod_spec(d, ctx_row),
                  pl.BlockSpec((d, 2 * hidden), const, pipeline_mode=pl.Buffered(1)),
                  pl.BlockSpec((hidden, d), const, pipeline_mode=pl.Buffered(1)),
                  pl.BlockSpec((1, d), const),
                  pl.BlockSpec((1, d), const)],
        out_specs=pl.BlockSpec((1, tm, d), lambda b, j: (b, j, 0)),
        out_shape=jax.ShapeDtypeStruct(x.shape, F32),
        compiler_params=_cparams("parallel", "parallel"),
        name="ffn",
    )(x, mods, w_gu, w_down, ln_g[None], ln_b[None])


def _proj_kernel(x_ref, m_ref, w_ref, *rest, chunks, rope_chunk):
    if rope_chunk is None:
        (o_ref,) = rest
    else:
        cos_ref, sin_ref, o_ref = rest
    x = x_ref[0]
    h = (x * (1.0 + m_ref[0, 4:5, :]) + m_ref[0, 3:4, :]).astype(BF16)
    for ci, (lo, hi) in enumerate(chunks):
        u = jnp.dot(h, w_ref[:, lo:hi], preferred_element_type=F32)
        if ci == rope_chunk:
            n = hi - lo
            first = (_iota((1, n), 1) & 16) == 0
            partner = jnp.where(first, pltpu.roll(u, n - 16, axis=1), pltpu.roll(u, 16, axis=1))
            u = u * cos_ref[...] + partner * sin_ref[...]
        o_ref[0, :, lo:hi] = u


def _proj(x, mods, w_in, chunks, *, tm, ctx_row=None, rope=None):
    bsz, t, d = x.shape
    n = w_in.shape[1]
    const = lambda b, j: (0, 0)
    in_specs = [pl.BlockSpec((1, tm, d), lambda b, j: (b, j, 0)),
                _mod_spec(d, ctx_row),
                pl.BlockSpec((d, n), const, pipeline_mode=pl.Buffered(1))]
    args = [x, mods, w_in]
    rope_chunk = None
    if rope is not None:
        rope_chunk, cos_t, sin_t = rope
        width = cos_t.shape[1]
        in_specs += [pl.BlockSpec((tm, width), lambda b, j: (j, 0))] * 2
        args += [cos_t, sin_t]
    return pl.pallas_call(
        functools.partial(_proj_kernel, chunks=chunks, rope_chunk=rope_chunk),
        grid=(bsz, t // tm),
        in_specs=in_specs,
        out_specs=pl.BlockSpec((1, tm, n), lambda b, j: (b, j, 0)),
        out_shape=jax.ShapeDtypeStruct((bsz, t, n), F32),
        compiler_params=_cparams("parallel", "parallel"),
        name="proj",
    )(*args)


def _mixout_kernel(x_ref, m_ref, ya_ref, yb_ref, w_ref, g_ref, b_ref, o_ref):
    x = x_ref[0]
    half = ya_ref.shape[2]
    y = _mm(ya_ref[0], w_ref[0:half, :]) + _mm(yb_ref[0], w_ref[half:2 * half, :])
    z = DEEPNORM_ALPHA * x + m_ref[0, 5:6, :] * y
    o_ref[0] = _layer_norm(z, g_ref[...], b_ref[...])


def _mixout(x, mods, ya, yb, w_out, ln_g, ln_b, *, tm, ctx_row=None):
    bsz, t, d = x.shape
    half = ya.shape[2]
    const = lambda b, j: (0, 0)
    tok = lambda b, j: (b, j, 0)
    return pl.pallas_call(
        _mixout_kernel,
        grid=(bsz, t // tm),
        in_specs=[pl.BlockSpec((1, tm, d), tok), _mod_spec(d, ctx_row),
                  pl.BlockSpec((1, tm, half), tok), pl.BlockSpec((1, tm, half), tok),
                  pl.BlockSpec((2 * half, d), const, pipeline_mode=pl.Buffered(1)),
                  pl.BlockSpec((1, d), const), pl.BlockSpec((1, d), const)],
        out_specs=pl.BlockSpec((1, tm, d), tok),
        out_shape=jax.ShapeDtypeStruct(x.shape, F32),
        compiler_params=_cparams("parallel", "parallel"),
        name="mixout",
    )(x, mods, ya, yb, w_out, ln_g[None], ln_b[None])


def _rwkv_prep_kernel(u_ref, up_ref, un_ref, mu_ref, w0_ref, w2_ref, a0_ref, a2_ref, g2_ref, kk_ref, ka_ref, rk_ref,
                      eh_ref, r_o, v_o, kk_o, w_o, k_o, kka_o, bonus_o, g_o, *, tt):
    j = pl.program_id(1)
    nj = pl.num_programs(1)
    wdt = RWKV_WIDTH
    u = u_ref[0]
    prev_row = up_ref[0, SUBLANES - 1:SUBLANES, :] * (j > 0).astype(F32)
    next_row = un_ref[0, 0:1, :] * (j < nj - 1).astype(F32)
    row = _iota((tt, 1), 0)
    up = jnp.where(row == 0, prev_row, pltpu.roll(u, 1, axis=0))
    un = jnp.where(row == tt - 1, next_row, pltpu.roll(u, tt - 1, axis=0))
    us = u + mu_ref[...] * (0.5 * (up + un) - u)
    r, k, v = us[:, 0:wdt], us[:, wdt:2 * wdt], us[:, 2 * wdt:3 * wdt]
    wd = jnp.tanh(us[:, 3 * wdt:3 * wdt + 128])
    ad = us[:, 3 * wdt + 128:3 * wdt + 256]
    gd = us[:, 3 * wdt + 256:3 * wdt + 384]
    eh = eh_ref[...]
    kkn = k * kk_ref[...]
    kkn = kkn / jnp.maximum(jnp.sqrt(_sel_right(kkn * kkn, eh)), 1e-12)
    bonus_o[0] = _sel_right(r * k * rk_ref[...], eh) * v
    g_o[0] = _mm(_sigmoid(gd), g2_ref[...])
    r_o[0] = r
    v_o[0] = v
    kk_o[0] = kkn
    for d in range(2):
        wl = w0_ref[d:d + 1, :] + _mm(wd, w2_ref[d])
        w_o[d, 0] = jnp.exp(-jnp.exp(_log_sigmoid(wl) - 0.5))
        a = _sigmoid(a0_ref[d:d + 1, :] + _mm(ad, a2_ref[d]))
        k_o[d, 0] = k * (1.0 + (a - 1.0) * ka_ref[...])
        kka_o[d, 0] = kkn * a


def _rwkv_prep(u, p, *, tt):
    bsz, t, _ = u.shape
    wdt = RWKV_WIDTH
    nblk8 = t // SUBLANES
    per = tt // SUBLANES
    c2 = lambda b, j: (0, 0)
    c3 = lambda b, j: (0, 0, 0)
    tok = lambda b, j: (b, j, 0)
    tok2 = lambda b, j: (0, b, j, 0)
    one = jax.ShapeDtypeStruct((bsz, t, wdt), F32)
    two = jax.ShapeDtypeStruct((2, bsz, t, wdt), F32)
    return pl.pallas_call(
        functools.partial(_rwkv_prep_kernel, tt=tt),
        grid=(bsz, t // tt),
        in_specs=[pl.BlockSpec((1, tt, RWKV_IN), tok),
                  pl.BlockSpec((1, SUBLANES, RWKV_IN), lambda b, j: (b, jnp.maximum(j * per - 1, 0), 0)),
                  pl.BlockSpec((1, SUBLANES, RWKV_IN), lambda b, j: (b, jnp.minimum((j + 1) * per, nblk8 - 1), 0)),
                  pl.BlockSpec((1, RWKV_IN), c2),
                  pl.BlockSpec((2, wdt), c2), pl.BlockSpec((2, 128, wdt), c3),
                  pl.BlockSpec((2, wdt), c2), pl.BlockSpec((2, 128, wdt), c3),
                  pl.BlockSpec((128, wdt), c2),
                  pl.BlockSpec((1, wdt), c2), pl.BlockSpec((1, wdt), c2), pl.BlockSpec((1, wdt), c2),
                  pl.BlockSpec((wdt, wdt), c2)],
        out_specs=[pl.BlockSpec((1, tt, wdt), tok)] * 3 + [pl.BlockSpec((2, 1, tt, wdt), tok2)] * 3
                  + [pl.BlockSpec((1, tt, wdt), tok)] * 2,
        out_shape=[one, one, one, two, two, two, one, one],
        compiler_params=_cparams("parallel", "parallel"),
        name="rwkv_prep",
    )(u, u, u, p["mu"], p["w0"], p["w2"], p["a0"], p["a2"], p["g2"], p["k_k"], p["k_a"], p["r_k"], p["eh"])


def _rwkv_scan_kernel(r_ref, v_ref, kk_ref, w_ref, k_ref, kka_ref, s0_ref, y_ref, sT_ref, s_ref, *, tb, k_unroll):
    d = pl.program_id(0)
    j = pl.program_id(1)
    nk = RWKV_HEAD_DIM
    nvb = RWKV_HEAD_DIM // SUBLANES
    lanes = s_ref.shape[1]

    @pl.when(j == 0)
    def _():
        s_ref[...] = s0_ref[...]

    def bcast(ref, t, k):
        return jnp.broadcast_to(ref[t, pl.ds(k, 1), :], (SUBLANES, lanes))

    def srow(k, vb):
        return pl.ds(pl.multiple_of(k * nk + vb * SUBLANES, SUBLANES), SUBLANES)

    def step(i, carry):
        t = jnp.where(d == 0, i, tb - 1 - i)

        def reduce_body(kc, acc):
            acc = list(acc)
            for kq in range(k_unroll):
                k = kc * k_unroll + kq
                kkb = bcast(kk_ref, t, k)
                wrb = bcast(w_ref, t, k) * bcast(r_ref, t, k)
                for vb in range(nvb):
                    s = s_ref[srow(k, vb), :]
                    acc[vb] = acc[vb] + s * kkb
                    acc[nvb + vb] = acc[nvb + vb] + s * wrb
            return tuple(acc)

        zero = jnp.zeros((SUBLANES, lanes), F32)
        acc = lax.fori_loop(0, nk // k_unroll, reduce_body, (zero,) * (2 * nvb))
        skk = acc[:nvb]
        rt = r_ref[t]
        c1 = jnp.sum(kka_ref[t] * rt, axis=0, keepdims=True)
        c2 = jnp.sum(k_ref[t] * rt, axis=0, keepdims=True)
        vt = [v_ref[t, vb * SUBLANES:(vb + 1) * SUBLANES, :] for vb in range(nvb)]
        for vb in range(nvb):
            y_ref[t, vb * SUBLANES:(vb + 1) * SUBLANES, :] = acc[nvb + vb] - skk[vb] * c1 + vt[vb] * c2

        def update_body(kc, c):
            for kq in range(k_unroll):
                k = kc * k_unroll + kq
                wb = bcast(w_ref, t, k)
                ab = bcast(kka_ref, t, k)
                kb = bcast(k_ref, t, k)
                for vb in range(nvb):
                    idx = srow(k, vb)
                    s_ref[idx, :] = s_ref[idx, :] * wb - skk[vb] * ab + vt[vb] * kb
            return c

        lax.fori_loop(0, nk // k_unroll, update_body, 0)
        return carry

    lax.fori_loop(0, tb, step, 0)

    @pl.when(j == pl.num_programs(1) - 1)
    def _():
        sT_ref[...] = s_ref[...]


def _rwkv_scan(r, v, kk, w, k, kka, s0, *, tb):
    t, nk, lanes = r.shape
    nblk = t // tb
    shared = pl.BlockSpec((tb, nk, lanes), lambda d, j: (j + d * (nblk - 1 - 2 * j), 0, 0))
    perdir = pl.BlockSpec((None, tb, nk, lanes), lambda d, j: (d, j + d * (nblk - 1 - 2 * j), 0, 0))
    state = pl.BlockSpec((None, nk * nk, lanes), lambda d, j: (d, 0, 0))
    return pl.pallas_call(
        functools.partial(_rwkv_scan_kernel, tb=tb, k_unroll=8),
        grid=(2, nblk),
        in_specs=[shared, shared, shared, perdir, perdir, perdir, state],
        out_specs=[perdir, state],
        out_shape=[jax.ShapeDtypeStruct((2, t, nk, lanes), F32), jax.ShapeDtypeStruct((2, nk * nk, lanes), F32)],
        scratch_shapes=[pltpu.VMEM((nk * nk, lanes), F32)],
        compiler_params=_cparams("arbitrary", "arbitrary"),
        name="rwkv_scan",
    )(r, v, kk, w, k, kka, s0)


def _rwkv_out_kernel(yf_ref, yb_ref, bonus_ref, g_ref, gng_ref, gnb_ref, em_ref, o_ref):
    y = yf_ref[0, 0] + yb_ref[0, 0]
    em = em_ref[...]
    dlt = y - _sel_right(y, em)
    var = _sel_right(dlt * dlt, em)
    yn = dlt * lax.rsqrt(var + RWKV_GN_EPS) * gng_ref[...] + gnb_ref[...]
    o_ref[0] = (yn + bonus_ref[0]) * g_ref[0]


def _rwkv_out(y2, bonus, g, gn_g, gn_b, em, *, tt):
    _, bsz, t, wdt = y2.shape
    c2 = lambda b, j: (0, 0)
    tok = lambda b, j: (b, j, 0)
    return pl.pallas_call(
        _rwkv_out_kernel,
        grid=(bsz, t // tt),
        in_specs=[pl.BlockSpec((1, 1, tt, wdt), lambda b, j: (0, b, j, 0)),
                  pl.BlockSpec((1, 1, tt, wdt), lambda b, j: (1, b, j, 0)),
                  pl.BlockSpec((1, tt, wdt), tok), pl.BlockSpec((1, tt, wdt), tok),
                  pl.BlockSpec((1, wdt), c2), pl.BlockSpec((1, wdt), c2), pl.BlockSpec((wdt, wdt), c2)],
        out_specs=pl.BlockSpec((1, tt, wdt), tok),
        out_shape=jax.ShapeDtypeStruct((bsz, t, wdt), F32),
        compiler_params=_cparams("parallel", "parallel"),
        name="rwkv_out",
    )(y2, y2, bonus, g, gn_g, gn_b, em)


def _to_time_major(a):
    *lead, bsz, t, _ = a.shape
    nl = len(lead)
    a = a.reshape(*lead, bsz, t, RWKV_HEADS, RWKV_HEAD_DIM)
    a = a.transpose(*range(nl), nl + 1, nl + 3, nl, nl + 2)
    return a.reshape(*lead, t, RWKV_HEAD_DIM, bsz * RWKV_HEADS)


def _from_time_major(a, bsz):
    two, t, n, _ = a.shape
    a = a.reshape(two, t, n, bsz, RWKV_HEADS).transpose(0, 3, 1, 4, 2)
    return a.reshape(two, bsz, t, RWKV_HEADS * n)


def _na_kernel(q_ref, k_ref, v_ref, qc_ref, kc_ref, vc_ref, bias_ref, y_ref, yc_ref, kb_s, vb_s, kcb_s, vcb_s, *, rows):
    scale = NA_HEAD_DIM ** -0.5
    wq = GRID_W
    band = NA_WIN_R * GRID_W
    kb_s[...] = k_ref[0].astype(BF16)
    vb_s[...] = v_ref[0].astype(BF16)
    kcb_s[...] = kc_ref[0].astype(BF16)
    vcb_s[...] = vc_ref[0].astype(BF16)
    low = _iota((1, LANES), 1) < NA_HEAD_DIM

    def stack_heads(qs):
        return jnp.concatenate([jnp.where(low, qs, 0.0), jnp.where(low, 0.0, qs)], axis=0).astype(BF16)

    def softmax_parts(scores):
        m = functools.reduce(jnp.maximum, [jnp.max(s, axis=-1, keepdims=True) for s in scores])
        ps = [jnp.exp(s - m) for s in scores]
        l = functools.reduce(lambda a, b: a + b, [jnp.sum(p, axis=-1, keepdims=True) for p in ps])
        return ps, l

    def combine(ps, l, values):
        o = functools.reduce(lambda a, b: a + b, [_mm(p, v) for p, v in zip(ps, values)]) / l
        m = o.shape[0] // 2
        return jnp.where(low, o[0:m], o[m:2 * m])

    def row_group(i, c):
        jobs = []
        for u in range(NA_ROW_GROUP):
            r = i * NA_ROW_GROUP + u
            rs = jnp.clip(r - NA_WIN_R // 2, 0, rows - NA_WIN_R)
            var = jnp.where(r < NA_WIN_R // 2, r,
                            jnp.where(r > rows - NA_WIN_R // 2, r - (rows - NA_WIN_R), NA_WIN_R // 2))
            q0 = pl.multiple_of(r * wq, wq)
            k0 = pl.multiple_of(rs * wq, wq)
            qst = stack_heads(q_ref[0, pl.ds(q0, wq), :] * scale)
            scores = [_mm_nt(qst, kb_s[pl.ds(k0, band), :]) + bias_ref[0, var], _mm_nt(qst, kcb_s[...])]
            jobs.append((q0, k0, scores))
        soft = [softmax_parts(scores) for _, _, scores in jobs]
        for (q0, k0, _), (ps, l) in zip(jobs, soft):
            y_ref[0, pl.ds(q0, wq), :] = combine(ps, l, [vb_s[pl.ds(k0, band), :], vcb_s[...]])
        return c

    lax.fori_loop(0, rows // NA_ROW_GROUP, row_group, 0)
    ps, l = softmax_parts([_mm_nt(stack_heads(qc_ref[0] * scale), kcb_s[...])])
    yc_ref[0] = combine(ps, l, [vcb_s[...]])


def _na_bias_table(rpb, rows):
    cols = jnp.arange(GRID_W)
    col_start = jnp.clip(cols - NA_WIN_C // 2, 0, GRID_W - NA_WIN_C)
    col_in = (cols[None, :] >= col_start[:, None]) & (cols[None, :] < col_start[:, None] + NA_WIN_C)
    dc_idx = jnp.clip(cols[None, :] - cols[:, None] + NA_WIN_C - 1, 0, 2 * NA_WIN_C - 2)
    rpb_cols = jnp.where(col_in[None, None], rpb[:, :, dc_idx], -jnp.inf)
    half = NA_WIN_R // 2
    rep_rows = list(range(half)) + [half] + list(range(rows - half + 1, rows))
    tiles = []
    for r in rep_rows:
        rs = min(max(r - half, 0), rows - NA_WIN_R)
        dr_idx = rs + jnp.arange(NA_WIN_R) - r + NA_WIN_R - 1
        t = rpb_cols[:, dr_idx]
        tiles.append(t.transpose(0, 2, 1, 3).reshape(NA_HEADS, GRID_W, NA_WIN_R * GRID_W))
    tab = jnp.stack(tiles, axis=1)
    tab = tab.reshape(NA_HEADS // 2, 2, len(rep_rows), GRID_W, NA_WIN_R * GRID_W).transpose(0, 2, 1, 3, 4)
    return tab.reshape(NA_HEADS // 2, len(rep_rows), 2 * GRID_W, NA_WIN_R * GRID_W)


def _na(u, uc, bias_tab, col0):
    bsz, s, _ = u.shape
    lc = uc.shape[1]
    rows = s // GRID_W
    assert rows >= 2 * NA_WIN_R and rows % NA_ROW_GROUP == 0
    qb, kb, vb = col0 // LANES, col0 // LANES + 4, col0 // LANES + 8
    band = NA_WIN_R * GRID_W
    lat = lambda cb: pl.BlockSpec((1, s, LANES), lambda b, p: (b, 0, cb + p))
    cx = lambda cb: pl.BlockSpec((1, lc, LANES), lambda b, p: (b, 0, cb + p))
    return pl.pallas_call(
        functools.partial(_na_kernel, rows=rows),
        grid=(bsz, NA_HEADS // 2),
        in_specs=[lat(qb), lat(kb), lat(vb), cx(qb), cx(kb), cx(vb),
                  pl.BlockSpec((1, NA_WIN_R, 2 * GRID_W, band), lambda b, p: (p, 0, 0, 0))],
        out_specs=[pl.BlockSpec((1, s, LANES), lambda b, p: (b, 0, p)),
                   pl.BlockSpec((1, lc, LANES), lambda b, p: (b, 0, p))],
        out_shape=[jax.ShapeDtypeStruct((bsz, s, NA_HEADS * NA_HEAD_DIM), F32),
                   jax.ShapeDtypeStruct((bsz, lc, NA_HEADS * NA_HEAD_DIM), F32)],
        scratch_shapes=[pltpu.VMEM((s, LANES), BF16), pltpu.VMEM((s, LANES), BF16),
                        pltpu.VMEM((lc, LANES), BF16), pltpu.VMEM((lc, LANES), BF16)],
        compiler_params=_cparams("parallel", "arbitrary"),
        name="na",
    )(u, u, u, uc, uc, uc, bias_tab)


def _gla_kernel(q_ref, k_ref, v_ref, r_ref, gd_ref, kc_ref, vc_ref, gdc_ref, g2_ref, gb_ref, ng_ref, y_ref,
                bc_s, qt_s, kt_s, dec_s, ktc_s, decc_s, st_s, of_s):
    s_len = q_ref.shape[1]
    l_len = kc_ref.shape[1]
    cs = GLA_SUB
    scale = GLA_DK ** -0.5
    ri = _iota((LANES, LANES), 0)
    ci = _iota((LANES, LANES), 1)
    sub_shift = cs.bit_length() - 1
    same_chunk = (ri >> sub_shift) == (ci >> sub_shift)
    lane = _iota((1, LANES), 1)
    head_lane = [lane < GLA_DK, lane >= GLA_DK]
    st_mask = (_iota((2 * GLA_DV, LANES), 0) >= GLA_DV) == (_iota((2 * GLA_DV, LANES), 1) >= GLA_DK)
    trow = _iota((cs, 1), 0)

    for d in range(2):
        fwd = d == 0
        tri = same_chunk & ((ci <= ri) if fwd else (ci >= ri))
        cum_sel = jnp.concatenate([tri.astype(F32), same_chunk.astype(F32)], axis=0).astype(BF16)

        def gates(gd, d=d, cum_sel=cum_sel):
            g = _log_sigmoid(_mm(gd, g2_ref[d, 0]) + gb_ref[d, 0]) / GLA_GATE_NORM
            y = _sel_left(cum_sel, g)
            return y[0:LANES], y[LANES:2 * LANES]

        def prep_lat(i, c):
            rows = pl.ds(pl.multiple_of(i * LANES, LANES), LANES)
            bc, tot = gates(gd_ref[0, rows, :])
            bc_s[rows, :] = bc
            qt_s[rows, :] = q_ref[0, rows, :] * scale * jnp.exp(bc)
            kt_s[rows, :] = k_ref[0, rows, :] * jnp.exp(tot - bc)
            dec_s[rows, :] = jnp.exp(tot)
            return c

        def prep_ctx(i, c):
            rows = pl.ds(pl.multiple_of(i * LANES, LANES), LANES)
            bc, tot = gates(gdc_ref[0, rows, :])
            ktc_s[rows, :] = kc_ref[0, rows, :] * jnp.exp(tot - bc)
            decc_s[rows, :] = jnp.exp(tot)
            return c

        lax.fori_loop(0, s_len // LANES, prep_lat, 0, unroll=2)
        lax.fori_loop(0, l_len // LANES, prep_ctx, 0, unroll=2)
        st_s[...] = jnp.zeros(st_s.shape, F32)

        grp = GLA_GROUP
        order = tuple(range(grp)) if fwd else tuple(range(grp - 1, -1, -1))

        def group_rows(i, n_groups, fwd=fwd):
            g = i if fwd else n_groups - 1 - i
            return [pl.ds(pl.multiple_of((g * grp + u) * cs, cs), cs) for u in range(grp)]

        def increment(kt, vv):
            return jnp.where(st_mask, _mm_tn(vv, kt), 0.0)

        def ctx_group(i, c, order=order):
            rows = group_rows(i, l_len // (cs * grp))
            incs = [increment(ktc_s[rows[u], :], vc_ref[0, rows[u], :]) for u in range(grp)]
            st = st_s[...]
            for u in order:
                st = st * decc_s[rows[u], :][0:1, :] + incs[u]
            st_s[...] = st
            return c

        def lat_group(i, c, fwd=fwd, order=order):
            rows = group_rows(i, s_len // (cs * grp))
            vvs = [v_ref[0, rows[u], :] for u in range(grp)]
            incs = [increment(kt_s[rows[u], :], vvs[u]) for u in range(grp)]
            st = st_s[...]
            inter = [None] * grp
            for u in order:
                inter[u] = _mm_nt(qt_s[rows[u], :], st)
                st = st * dec_s[rows[u], :][0:1, :] + incs[u]
            st_s[...] = st
            for u in range(grp):
                vv = vvs[u]
                bc = bc_s[rows[u], :]
                qs = q_ref[0, rows[u], :] * scale
                kk = k_ref[0, rows[u], :]
                o0, o1 = inter[u][:, 0:GLA_DV], inter[u][:, GLA_DV:2 * GLA_DV]
                for s in range(cs):
                    keep = (trow >= s) if fwd else (trow <= s)
                    e = jnp.exp(jnp.where(keep, bc - bc[s:s + 1, :], -jnp.inf))
                    term = qs * kk[s:s + 1, :] * e
                    a0 = jnp.sum(jnp.where(head_lane[0], term, 0.0), axis=-1, keepdims=True)
                    a1 = jnp.sum(jnp.where(head_lane[1], term, 0.0), axis=-1, keepdims=True)
                    o0 = o0 + a0 * vv[s:s + 1, 0:GLA_DV]
                    o1 = o1 + a1 * vv[s:s + 1, GLA_DV:2 * GLA_DV]
                if fwd:
                    of_s[rows[u], 0:GLA_DV] = o0
                    of_s[rows[u], GLA_DV:2 * GLA_DV] = o1
                else:
                    rr = r_ref[0, rows[u], :]
                    for h, oh in enumerate((o0, o1)):
                        cols = slice(h * GLA_DV, (h + 1) * GLA_DV)
                        ot = of_s[rows[u], cols] + oh
                        on = ot * lax.rsqrt(jnp.mean(ot * ot, axis=-1, keepdims=True) + RMS_EPS) * ng_ref[...]
                        rh = rr[:, cols]
                        y_ref[0, rows[u], cols] = on * (rh * _sigmoid(rh))
            return c

        lax.fori_loop(0, l_len // (cs * grp), ctx_group, 0)
        lax.fori_loop(0, s_len // (cs * grp), lat_group, 0)


def _gla(u, uc, g2p, gbp, norm_g, gd_block):
    bsz, s, _ = u.shape
    lc = uc.shape[1]
    pairs = GLA_HEADS // 2
    dv2 = 2 * GLA_DV
    lat = lambda width, blk: pl.BlockSpec((1, s, width), lambda b, p: (b, 0, blk(p)))
    cx = lambda width, blk: pl.BlockSpec((1, lc, width), lambda b, p: (b, 0, blk(p)))
    q_blk = lambda p: p
    k_blk = lambda p: pairs + p
    v_blk = lambda p: 2 * pairs * LANES // dv2 + p
    r_blk = lambda p: (2 * pairs * LANES + GLA_HEADS * GLA_DV) // dv2 + p
    gd_blk = lambda p: gd_block
    return pl.pallas_call(
        _gla_kernel,
        grid=(bsz, pairs),
        in_specs=[lat(LANES, q_blk), lat(LANES, k_blk), lat(dv2, v_blk), lat(dv2, r_blk), lat(LANES, gd_blk),
                  cx(LANES, k_blk), cx(dv2, v_blk), cx(LANES, gd_blk),
                  pl.BlockSpec((2, 1, LANES, LANES), lambda b, p: (0, p, 0, 0)),
                  pl.BlockSpec((2, 1, 1, LANES), lambda b, p: (0, p, 0, 0)),
                  pl.BlockSpec((1, GLA_DV), lambda b, p: (0, 0))],
        out_specs=pl.BlockSpec((1, s, dv2), lambda b, p: (b, 0, p)),
        out_shape=jax.ShapeDtypeStruct((bsz, s, GLA_HEADS * GLA_DV), F32),
        scratch_shapes=[pltpu.VMEM((s, LANES), F32)] * 4 + [pltpu.VMEM((lc, LANES), F32)] * 2
                       + [pltpu.VMEM((dv2, LANES), F32), pltpu.VMEM((s, dv2), F32)],
        compiler_params=_cparams("parallel", "arbitrary"),
        name="gla",
    )(u, u, u, u, u, uc, uc, uc, g2p, gbp, norm_g)


def _diff_kernel(q_ref, k_ref, v_ref, kc_ref, vc_ref, lam_ref, ng_ref, y_ref, k_s, v_s, *, lambda_init):
    scale = DIFF_DH ** -0.5
    s_len = k_ref.shape[1]

    @pl.when(pl.program_id(2) == 0)
    def _():
        k_s[0:s_len, :] = k_ref[0].astype(BF16)
        v_s[0:s_len, :] = v_ref[0].astype(BF16)
        k_s[s_len:, :] = kc_ref[0].astype(BF16)
        v_s[s_len:, :] = vc_ref[0].astype(BF16)

    lp = lam_ref[...]
    lam = (jnp.exp(jnp.sum(lp[0:1] * lp[1:2], axis=-1, keepdims=True))
           - jnp.exp(jnp.sum(lp[2:3] * lp[3:4], axis=-1, keepdims=True)) + lambda_init)
    qs = q_ref[0] * scale
    low = _iota((1, LANES), 1) < DIFF_DH
    outs = []
    for m in range(2):
        qm = jnp.where(low, qs, 0.0) if m == 0 else jnp.where(low, 0.0, qs)
        s = _mm_nt(qm, k_s[...])
        p = jnp.exp(s - jnp.max(s, axis=-1, keepdims=True))
        outs.append(_mm(p, v_s[...]) / jnp.sum(p, axis=-1, keepdims=True))
    o = outs[0] - lam * outs[1]
    on = o * lax.rsqrt(jnp.mean(o * o, axis=-1, keepdims=True) + RMS_EPS) * ng_ref[...]
    y_ref[0] = on * (1.0 - lambda_init)


def _diff(u, uc, lam_params, norm_g, col0, lambda_init, *, tq):
    bsz, s, _ = u.shape
    lc = uc.shape[1]
    qb = col0 // LANES
    kb, vb = qb + DIFF_HEADS, qb + 2 * DIFF_HEADS
    full = lambda arr_len, cb: pl.BlockSpec((1, arr_len, LANES), lambda b, h, j: (b, 0, cb + h))
    return pl.pallas_call(
        functools.partial(_diff_kernel, lambda_init=lambda_init),
        grid=(bsz, DIFF_HEADS, s // tq),
        in_specs=[pl.BlockSpec((1, tq, LANES), lambda b, h, j: (b, j, qb + h)),
                  full(s, kb), full(s, vb), full(lc, kb), full(lc, vb),
                  pl.BlockSpec((4, DIFF_DH), lambda b, h, j: (0, 0)),
                  pl.BlockSpec((1, DIFF_DV), lambda b, h, j: (0, 0))],
        out_specs=pl.BlockSpec((1, tq, LANES), lambda b, h, j: (b, j, h)),
        out_shape=jax.ShapeDtypeStruct((bsz, s, DIFF_HEADS * DIFF_DV), F32),
        scratch_shapes=[pltpu.VMEM((s + lc, LANES), BF16), pltpu.VMEM((s + lc, LANES), BF16)],
        compiler_params=_cparams("parallel", "parallel", "arbitrary"),
        name="diff_attn",
    )(u, u, u, uc, uc, lam_params, norm_g)


def _head_selector(width, head_dim, value):
    idx = jnp.arange(width) // head_dim
    return jnp.where(idx[:, None] == idx[None, :], value, 0.0).astype(F32)


def _pad_rows(w, lo, total):
    return jnp.zeros((total, w.shape[1]), w.dtype).at[lo:lo + w.shape[0]].set(w)


def _rope_tables(s, reps):
    pos = jnp.arange(s)
    n = DIFF_DH // 4
    freqs = ROPE_BASE ** (-jnp.arange(n, dtype=F32) / n)
    ang_r = (pos // GRID_W).astype(F32)[:, None] * freqs[None, :]
    ang_c = (pos % GRID_W).astype(F32)[:, None] * freqs[None, :]
    cos = jnp.concatenate([jnp.cos(ang_r)] * 2 + [jnp.cos(ang_c)] * 2, axis=-1)
    sin = jnp.concatenate([-jnp.sin(ang_r), jnp.sin(ang_r), -jnp.sin(ang_c), jnp.sin(ang_c)], axis=-1)
    return jnp.tile(cos, (1, reps)), jnp.tile(sin, (1, reps))


def _rwkv_na_mixer(xl, xc, mods, ctx_row, w_in, mu, w0, w2, a0, a2, g2, k_k, k_a, r_k, gn_g, gn_b, rpb):
    bsz, s, _ = xl.shape
    lc = xc.shape[1]
    n = w_in.shape[1]
    chunks = tuple((lo, min(lo + 512, n)) for lo in range(0, n, 512))
    u = _proj(xl, mods, w_in, chunks, tm=512)
    uc = _proj(xc, mods, w_in, chunks, tm=lc, ctx_row=ctx_row)

    eh = _head_selector(RWKV_WIDTH, RWKV_HEAD_DIM, 1.0).astype(BF16)
    em = _head_selector(RWKV_WIDTH, RWKV_HEAD_DIM, 1.0 / RWKV_HEAD_DIM).astype(BF16)
    p = dict(mu=mu[None], w0=w0, a0=a0, g2=g2.astype(BF16), k_k=k_k[None], k_a=k_a[None], r_k=r_k.reshape(1, -1), eh=eh,
             w2=jnp.stack([_pad_rows(w2[0], 0, 128), _pad_rows(w2[1], 64, 128)]).astype(BF16),
             a2=jnp.stack([_pad_rows(a2[0], 0, 128), _pad_rows(a2[1], 64, 128)]).astype(BF16))
    lanes = bsz * RWKV_HEADS
    state = jnp.zeros((2, RWKV_HEAD_DIM * RWKV_HEAD_DIM, lanes), F32)
    ys = []
    for uu, tt, tb in ((uc, lc, 16), (u, 256, 16)):
        r, v, kk, w, kd, kka, bonus, g = _rwkv_prep(uu, p, tt=tt)
        y2, state = _rwkv_scan(_to_time_major(r), _to_time_major(v), _to_time_major(kk), _to_time_major(w),
                               _to_time_major(kd), _to_time_major(kka), state, tb=tb)
        ys.append(_rwkv_out(_from_time_major(y2, bsz), bonus, g, gn_g[None], gn_b[None], em, tt=tt))
    yc_a, y_a = ys
    y_b, yc_b = _na(u, uc, _na_bias_table(rpb, s // GRID_W), RWKV_IN)
    return (y_a, y_b), (yc_a, yc_b)


def _gla_diff_mixer(xl, xc, mods, ctx_row, w_in, gla_g2, gla_gb, gla_norm_g, diff_lambda, diff_norm_g, lambda_init):
    bsz, s, d = xl.shape
    lc = xc.shape[1]
    n_gla = 2 * GLA_HEADS * GLA_DK + 2 * GLA_HEADS * GLA_DV
    n_gate = 2 * GLA_GATE_LORA
    n_diff = 2 * DIFF_HEADS * 2 * DIFF_DH + DIFF_HEADS * DIFF_DV
    w_perm = jnp.concatenate([w_in[:, :n_gla], w_in[:, n_gla + n_gate:n_gla + n_gate + n_diff],
                              w_in[:, n_gla:n_gla + n_gate], jnp.zeros((d, LANES - n_gate), w_in.dtype)], axis=1).astype(BF16)
    rope_w = 2 * DIFF_HEADS * 2 * DIFF_DH
    chunks = ((0, 512), (512, 1024), (1024, n_gla), (n_gla, n_gla + rope_w),
              (n_gla + rope_w, n_gla + n_diff), (n_gla + n_diff, n_gla + n_diff + LANES))
    cos_t, sin_t = _rope_tables(s, rope_w // DIFF_DH)
    u = _proj(xl, mods, w_perm, chunks, tm=512, rope=(3, cos_t, sin_t))
    uc = _proj(xc, mods, w_perm, chunks, tm=lc, ctx_row=ctx_row)

    pairs = GLA_HEADS // 2
    g2p = jnp.stack([jnp.stack([_pad_rows(gla_g2[dd][:, p * LANES:(p + 1) * LANES], dd * GLA_GATE_LORA, LANES)
                                for p in range(pairs)]) for dd in range(2)]).astype(BF16)
    gbp = gla_gb.reshape(2, pairs, 1, LANES)
    y_c = _gla(u, uc, g2p, gbp, gla_norm_g[None], (n_gla + n_diff) // LANES)
    y_d = _diff(u, uc, diff_lambda, diff_norm_g[None], n_gla, lambda_init, tq=512)
    return y_c, y_d


def kernel(x, c, ctx, c_ctx, w_mod_0, b_mod_0, ln_g_0, ln_b_0, ffn_gu_0, ffn_down_0, w_in_0, w_out_0, rwkv_mu_0, rwkv_w0_0, rwkv_w2_0, rwkv_a0_0, rwkv_a2_0, rwkv_g2_0, rwkv_k_k_0, rwkv_k_a_0, rwkv_r_k_0, rwkv_gn_g_0, rwkv_gn_b_0, na_rpb_0, w_mod_1, b_mod_1, ln_g_1, ln_b_1, ffn_gu_1, ffn_down_1, w_in_1, w_out_1, gla_g2_1, gla_gb_1, gla_norm_g_1, diff_lambda_1, diff_norm_g_1):
    bsz, s, d = x.shape
    lc = ctx.shape[1]
    ctx_row = bsz
    rows = -(-(bsz + 1) // SUBLANES) * SUBLANES
    c_all = jnp.concatenate([c, c_ctx[None], jnp.zeros((rows - bsz - 1, d), F32)], axis=0)
    tm = 512

    mods = _mods(c_all, w_mod_0, b_mod_0)
    gu, dn = ffn_gu_0.astype(BF16), ffn_down_0.astype(BF16)
    xl = _ffn(x, mods, 0, gu[0], dn[0], ln_g_0[0], ln_b_0[0], tm=tm)
    xc = _ffn(ctx, mods, 0, gu[0], dn[0], ln_g_0[0], ln_b_0[0], tm=lc, ctx_row=ctx_row)
    (y_a, y_b), (yc_a, yc_b) = _rwkv_na_mixer(xl, xc, mods, ctx_row, w_in_0.astype(BF16), rwkv_mu_0, rwkv_w0_0, rwkv_w2_0,
                                              rwkv_a0_0, rwkv_a2_0, rwkv_g2_0, rwkv_k_k_0, rwkv_k_a_0, rwkv_r_k_0,
                                              rwkv_gn_g_0, rwkv_gn_b_0, na_rpb_0)
    wo = w_out_0.astype(BF16)
    xl = _mixout(xl, mods, y_a, y_b, wo, ln_g_0[1], ln_b_0[1], tm=tm)
    xc = _mixout(xc, mods, yc_a, yc_b, wo, ln_g_0[1], ln_b_0[1], tm=lc, ctx_row=ctx_row)
    xl = _ffn(xl, mods, 6, gu[1], dn[1], ln_g_0[2], ln_b_0[2], tm=tm)
    xc = _ffn(xc, mods, 6, gu[1], dn[1], ln_g_0[2], ln_b_0[2], tm=lc, ctx_row=ctx_row)

    mods = _mods(c_all, w_mod_1, b_mod_1)
    gu, dn = ffn_gu_1.astype(BF16), ffn_down_1.astype(BF16)
    xl = _ffn(xl, mods, 0, gu[0], dn[0], ln_g_1[0], ln_b_1[0], tm=tm)
    xc = _ffn(xc, mods, 0, gu[0], dn[0], ln_g_1[0], ln_b_1[0], tm=lc, ctx_row=ctx_row)
    y_c, y_d = _gla_diff_mixer(xl, xc, mods, ctx_row, w_in_1, gla_g2_1, gla_gb_1, gla_norm_g_1, diff_lambda_1, diff_norm_g_1,
                               0.8 - 0.6 * math.exp(-0.3 * 1))
    xl = _mixout(xl, mods, y_c, y_d, w_out_1.astype(BF16), ln_g_1[1], ln_b_1[1], tm=tm)
    xl = _ffn(xl, mods, 6, gu[1], dn[1], ln_g_1[2], ln_b_1[2], tm=tm)
    return xl
```

```python
import functools
import math

import jax
import jax.numpy as jnp
from jax import lax
from jax.experimental import pallas as pl
from jax.experimental.pallas import tpu as pltpu

F32 = jnp.float32
BF16 = jnp.bfloat16

GRID_W = 64
N_MOD = 9
DEPTH = 2
LN_EPS = 1e-5
RMS_EPS = 1e-6
DEEPNORM_ALPHA = (2 * DEPTH) ** 0.25
RWKV_HEADS = 8
RWKV_HEAD_DIM = 64
RWKV_WIDTH = RWKV_HEADS * RWKV_HEAD_DIM
RWKV_IN = 3 * RWKV_WIDTH + 4 * 64 + 128
RWKV_GN_EPS = 64e-5
RWKV_BLOCK = 16
NA_HEADS = 8
NA_HEAD_DIM = 64
NA_WIN_R = 8
NA_WIN_C = 16
NA_ROW_GROUP = 2
GLA_HEADS = 4
GLA_DK = 64
GLA_DV = 128
GLA_GATE_LORA = 16
GLA_GATE_NORM = 16.0
GLA_SUB = 16
GLA_GROUP = 4
DIFF_HEADS = 4
DIFF_DH = 64
DIFF_DV = 128
ROPE_BASE = 10000.0

LANES = 128
SUBLANES = 8
V7X_VMEM_BYTES = 64 * 1024 * 1024
VMEM_LIMIT = (V7X_VMEM_BYTES * 7) // 8


def _cparams(*sem):
    return pltpu.CompilerParams(dimension_semantics=sem, vmem_limit_bytes=VMEM_LIMIT)


def _mm(a, b):
    return jnp.dot(a.astype(BF16), b.astype(BF16), preferred_element_type=F32)


def _mm_nt(a, b):
    return lax.dot_general(a.astype(BF16), b.astype(BF16), (((1,), (1,)), ((), ())), preferred_element_type=F32)


def _mm_tn(a, b):
    return lax.dot_general(a.astype(BF16), b.astype(BF16), (((0,), (0,)), ((), ())), preferred_element_type=F32)


def _split3(x):
    hi = x.astype(BF16)
    r1 = x - hi.astype(F32)
    mid = r1.astype(BF16)
    lo = (r1 - mid.astype(F32)).astype(BF16)
    return hi, mid, lo


def _sel_right(x, sel):
    m = x.shape[0]
    y = jnp.dot(jnp.concatenate(_split3(x), axis=0), sel, preferred_element_type=F32)
    return y[0:m] + y[m:2 * m] + y[2 * m:3 * m]


def _sel_left(sel, x):
    n = x.shape[1]
    y = jnp.dot(sel, jnp.concatenate(_split3(x), axis=1), preferred_element_type=F32)
    return y[:, 0:n] + y[:, n:2 * n] + y[:, 2 * n:3 * n]


def _sigmoid(x):
    return 1.0 / (1.0 + jnp.exp(-x))


def _log_sigmoid(x):
    return -(jnp.maximum(-x, 0.0) + jnp.log1p(jnp.exp(-jnp.abs(x))))


def _layer_norm(z, g, b):
    mu = jnp.mean(z, axis=-1, keepdims=True)
    d = z - mu
    var = jnp.mean(d * d, axis=-1, keepdims=True)
    return d * lax.rsqrt(var + LN_EPS) * g + b


def _iota(shape, dim):
    return lax.broadcasted_iota(jnp.int32, shape, dim)


def _mods_kernel(c_ref, w_ref, b_ref, o_ref):
    c = c_ref[...]
    o_ref[...] = _mm(c * _sigmoid(c), w_ref[...]) + b_ref[...]


def _mods(c_all, w_mod, b_mod):
    rows, d = c_all.shape
    n = w_mod.shape[1]
    tn = n // 8
    out = pl.pallas_call(
        _mods_kernel,
        grid=(n // tn,),
        in_specs=[pl.BlockSpec((rows, d), lambda j: (0, 0)),
                  pl.BlockSpec((d, tn), lambda j: (0, j)),
                  pl.BlockSpec((1, tn), lambda j: (0, j))],
        out_specs=pl.BlockSpec((rows, tn), lambda j: (0, j)),
        out_shape=jax.ShapeDtypeStruct((rows, n), F32),
        compiler_params=_cparams("arbitrary"),
        name="mods",
    )(c_all, w_mod, b_mod[None])
    return out.reshape(rows, N_MOD, d)


def _mod_spec(d, ctx_row):
    if ctx_row is None:
        return pl.BlockSpec((1, N_MOD, d), lambda b, j: (b, 0, 0))
    return pl.BlockSpec((1, N_MOD, d), lambda b, j: (ctx_row, 0, 0))


def _ffn_kernel(x_ref, m_ref, wgu_ref, wd_ref, g_ref, b_ref, o_ref, *, i0, fc):
    hidden = wd_ref.shape[0]
    x = x_ref[0]
    shift, scale, gate = (m_ref[0, i0 + i:i0 + i + 1, :] for i in range(3))
    h = (x * (1.0 + scale) + shift).astype(BF16)
    acc = None
    for c in range(hidden // fc):
        g = jnp.dot(h, wgu_ref[:, c * fc:(c + 1) * fc], preferred_element_type=F32)
        u = jnp.dot(h, wgu_ref[:, hidden + c * fc:hidden + (c + 1) * fc], preferred_element_type=F32)
        a = (g * _sigmoid(g) * u).astype(BF16)
        dn = jnp.dot(a, wd_ref[c * fc:(c + 1) * fc, :], preferred_element_type=F32)
        acc = dn if acc is None else acc + dn
    z = DEEPNORM_ALPHA * x + gate * (0.5 * acc)
    o_ref[0] = _layer_norm(z, g_ref[...], b_ref[...])


def _ffn(x, mods, i0, w_gu, w_down, ln_g, ln_b, *, tm, ctx_row=None):
    bsz, t, d = x.shape
    hidden = w_down.shape[0]
    const = lambda b, j: (0, 0)
    return pl.pallas_call(
        functools.partial(_ffn_kernel, i0=i0, fc=256),
        grid=(bsz, t // tm),
        in_specs=[pl.BlockSpec((1, tm, d), lambda b, j: (b, j, 0)),
                  _mod_spec(d, ctx_row),
                  pl.BlockSpec((d, 2 * hidden), const, pipeline_mode=pl.Buffered(1)),
                  pl.BlockSpec((hidden, d), const, pipeline_mode=pl.Buffered(1)),
                  pl.BlockSpec((1, d), const),
                  pl.BlockSpec((1, d), const)],
        out_specs=pl.BlockSpec((1, tm, d), lambda b, j: (b, j, 0)),
        out_shape=jax.ShapeDtypeStruct(x.shape, F32),
        compiler_params=_cparams("parallel", "parallel"),
        name="ffn",
    )(x, mods, w_gu, w_down, ln_g[None], ln_b[None])


def _proj_kernel(x_ref, m_ref, w_ref, *rest, chunks, rope_chunk):
    if rope_chunk is None:
        (o_ref,) = rest
    else:
        cos_ref, sin_ref, o_ref = rest
    x = x_ref[0]
    h = (x * (1.0 + m_ref[0, 4:5, :]) + m_ref[0, 3:4, :]).astype(BF16)
    for ci, (lo, hi) in enumerate(chunks):
        u = jnp.dot(h, w_ref[:, lo:hi], preferred_element_type=F32)
        if ci == rope_chunk:
            n = hi - lo
            first = (_iota((1, n), 1) & 16) == 0
            partner = jnp.where(first, pltpu.roll(u, n - 16, axis=1), pltpu.roll(u, 16, axis=1))
            u = u * cos_ref[...] + partner * sin_ref[...]
        o_ref[0, :, lo:hi] = u


def _proj(x, mods, w_in, chunks, *, tm, ctx_row=None, rope=None):
    bsz, t, d = x.shape
    n = w_in.shape[1]
    const = lambda b, j: (0, 0)
    in_specs = [pl.BlockSpec((1, tm, d), lambda b, j: (b, j, 0)),
                _mod_spec(d, ctx_row),
                pl.BlockSpec((d, n), const, pipeline_mode=pl.Buffered(1))]
    args = [x, mods, w_in]
    rope_chunk = None
    if rope is not None:
        rope_chunk, cos_t, sin_t = rope
        width = cos_t.shape[1]
        in_specs += [pl.BlockSpec((tm, width), lambda b, j: (j, 0))] * 2
        args += [cos_t, sin_t]
    return pl.pallas_call(
        functools.partial(_proj_kernel, chunks=chunks, rope_chunk=rope_chunk),
        grid=(bsz, t // tm),
        in_specs=in_specs,
        out_specs=pl.BlockSpec((1, tm, n), lambda b, j: (b, j, 0)),
        out_shape=jax.ShapeDtypeStruct((bsz, t, n), F32),
        compiler_params=_cparams("parallel", "parallel"),
        name="proj",
    )(*args)


def _mixout_kernel(x_ref, m_ref, ya_ref, yb_ref, w_ref, g_ref, b_ref, o_ref):
    x = x_ref[0]
    half = ya_ref.shape[2]
    y = _mm(ya_ref[0], w_ref[0:half, :]) + _mm(yb_ref[0], w_ref[half:2 * half, :])
    z = DEEPNORM_ALPHA * x + m_ref[0, 5:6, :] * y
    o_ref[0] = _layer_norm(z, g_ref[...], b_ref[...])


def _mixout(x, mods, ya, yb, w_out, ln_g, ln_b, *, tm, ctx_row=None):
    bsz, t, d = x.shape
    half = ya.shape[2]
    const = lambda b, j: (0, 0)
    tok = lambda b, j: (b, j, 0)
    return pl.pallas_call(
        _mixout_kernel,
        grid=(bsz, t // tm),
        in_specs=[pl.BlockSpec((1, tm, d), tok), _mod_spec(d, ctx_row),
                  pl.BlockSpec((1, tm, half), tok), pl.BlockSpec((1, tm, half), tok),
                  pl.BlockSpec((2 * half, d), const, pipeline_mode=pl.Buffered(1)),
                  pl.BlockSpec((1, d), const), pl.BlockSpec((1, d), const)],
        out_specs=pl.BlockSpec((1, tm, d), tok),
        out_shape=jax.ShapeDtypeStruct(x.shape, F32),
        compiler_params=_cparams("parallel", "parallel"),
        name="mixout",
    )(x, mods, ya, yb, w_out, ln_g[None], ln_b[None])


def _rwkv_prep_kernel(u_ref, up_ref, un_ref, mu_ref, w0_ref, w2_ref, a0_ref, a2_ref, g2_ref, kk_ref, ka_ref, rk_ref,
                      eh_ref, cum_ref, v_o, a1_o, a2_o, b1_o, b2_o, wend_o, bonus_o, g_o, wfull_s, *, tt):
    j = pl.program_id(1)
    nj = pl.num_programs(1)
    wdt = RWKV_WIDTH
    u = u_ref[0]
    prev_row = up_ref[0, SUBLANES - 1:SUBLANES, :] * (j > 0).astype(F32)
    next_row = un_ref[0, 0:1, :] * (j < nj - 1).astype(F32)
    row = _iota((tt, 1), 0)
    up = jnp.where(row == 0, prev_row, pltpu.roll(u, 1, axis=0))
    un = jnp.where(row == tt - 1, next_row, pltpu.roll(u, tt - 1, axis=0))
    us = u + mu_ref[...] * (0.5 * (up + un) - u)
    r, k, v = us[:, 0:wdt], us[:, wdt:2 * wdt], us[:, 2 * wdt:3 * wdt]
    wd = jnp.tanh(us[:, 3 * wdt:3 * wdt + 128])
    ad = us[:, 3 * wdt + 128:3 * wdt + 256]
    gd = us[:, 3 * wdt + 256:3 * wdt + 384]
    eh = eh_ref[...]
    kkn = k * kk_ref[...]
    kkn = kkn / jnp.maximum(jnp.sqrt(_sel_right(kkn * kkn, eh)), 1e-12)
    bonus_o[0] = _sel_right(r * k * rk_ref[...], eh) * v
    g_o[0] = _mm(_sigmoid(gd), g2_ref[...])
    v_o[0] = v
    blk = RWKV_BLOCK
    for d in range(2):
        wl = w0_ref[d:d + 1, :] + _mm(wd, w2_ref[d])
        e = jnp.exp(_log_sigmoid(wl) - 0.5)
        e_inc = _sel_left(cum_ref[d], e)
        w_inc = jnp.exp(-e_inc)
        inv_w = jnp.exp(e_inc)
        a = _sigmoid(a0_ref[d:d + 1, :] + _mm(ad, a2_ref[d]))
        a1_o[d, 0] = kkn * jnp.exp(e - e_inc)
        a2_o[d, 0] = r * w_inc
        b1_o[d, 0] = kkn * a * inv_w
        b2_o[d, 0] = k * (1.0 + (a - 1.0) * ka_ref[...]) * inv_w
        last = blk - 1 if d == 0 else 0
        for c in range(wdt // LANES):
            cols = slice(c * LANES, (c + 1) * LANES)
            wfull_s[c] = w_inc[:, cols]
            wend_o[d, 0, :, cols] = wfull_s[c, pl.ds(last, tt // blk, stride=blk), :]


def _rwkv_prep(u, p, *, tt):
    bsz, t, _ = u.shape
    wdt = RWKV_WIDTH
    nblk8 = t // SUBLANES
    per = tt // SUBLANES
    c2 = lambda b, j: (0, 0)
    c3 = lambda b, j: (0, 0, 0)
    tok = lambda b, j: (b, j, 0)
    tok2 = lambda b, j: (0, b, j, 0)
    one = jax.ShapeDtypeStruct((bsz, t, wdt), F32)
    two = jax.ShapeDtypeStruct((2, bsz, t, wdt), F32)
    blk = RWKV_BLOCK
    ends = jax.ShapeDtypeStruct((2, bsz, t // blk, wdt), F32)
    ri = jnp.arange(tt)[:, None]
    ci = jnp.arange(tt)[None, :]
    same = (ri // blk) == (ci // blk)
    cum_sel = jnp.stack([same & (ci <= ri), same & (ci >= ri)]).astype(BF16)
    return pl.pallas_call(
        functools.partial(_rwkv_prep_kernel, tt=tt),
        grid=(bsz, t // tt),
        in_specs=[pl.BlockSpec((1, tt, RWKV_IN), tok),
                  pl.BlockSpec((1, SUBLANES, RWKV_IN), lambda b, j: (b, jnp.maximum(j * per - 1, 0), 0)),
                  pl.BlockSpec((1, SUBLANES, RWKV_IN), lambda b, j: (b, jnp.minimum((j + 1) * per, nblk8 - 1), 0)),
                  pl.BlockSpec((1, RWKV_IN), c2),
                  pl.BlockSpec((2, wdt), c2), pl.BlockSpec((2, 128, wdt), c3),
                  pl.BlockSpec((2, wdt), c2), pl.BlockSpec((2, 128, wdt), c3),
                  pl.BlockSpec((128, wdt), c2),
                  pl.BlockSpec((1, wdt), c2), pl.BlockSpec((1, wdt), c2), pl.BlockSpec((1, wdt), c2),
                  pl.BlockSpec((wdt, wdt), c2), pl.BlockSpec((2, tt, tt), c3)],
        out_specs=[pl.BlockSpec((1, tt, wdt), tok)] + [pl.BlockSpec((2, 1, tt, wdt), tok2)] * 4
                  + [pl.BlockSpec((2, 1, tt // blk, wdt), tok2)] + [pl.BlockSpec((1, tt, wdt), tok)] * 2,
        out_shape=[one, two, two, two, two, ends, one, one],
        scratch_shapes=[pltpu.VMEM((wdt // LANES, tt, LANES), F32)],
        compiler_params=_cparams("parallel", "parallel"),
        name="rwkv_prep",
    )(u, u, u, p["mu"], p["w0"], p["w2"], p["a0"], p["a2"], p["g2"], p["k_k"], p["k_a"], p["r_k"], p["eh"], cum_sel)


def _rwkv_scan_kernel(v_ref, a1_ref, a2_ref, b1_ref, b2_ref, wend_ref, s0_ref, y_ref, sT_ref, s_ref, *, k_unroll):
    d = pl.program_id(0)
    j = pl.program_id(1)
    tb = RWKV_BLOCK
    nk = RWKV_HEAD_DIM
    nvb = RWKV_HEAD_DIM // SUBLANES
    lanes = s_ref.shape[1]

    @pl.when(j == 0)
    def _():
        s_ref[...] = s0_ref[...]

    def bcast(ref, t, k):
        return jnp.broadcast_to(ref[t, pl.ds(k, 1), :], (SUBLANES, lanes))

    def srow(k, vb):
        return pl.ds(pl.multiple_of(k * nk + vb * SUBLANES, SUBLANES), SUBLANES)

    def step(i, carry):
        t = jnp.where(d == 0, i, tb - 1 - i)

        def reduce_body(kc, acc):
            acc = list(acc)
            for kq in range(k_unroll):
                k = kc * k_unroll + kq
                a1b = bcast(a1_ref, t, k)
                a2b = bcast(a2_ref, t, k)
                for vb in range(nvb):
                    s = s_ref[srow(k, vb), :]
                    acc[vb] = acc[vb] + s * a1b
                    acc[nvb + vb] = acc[nvb + vb] + s * a2b
            return tuple(acc)

        zero = jnp.zeros((SUBLANES, lanes), F32)
        acc = lax.fori_loop(0, nk // k_unroll, reduce_body, (zero,) * (2 * nvb))
        skk = acc[:nvb]
        a2t = a2_ref[t]
        c1 = jnp.sum(b1_ref[t] * a2t, axis=0, keepdims=True)
        c2 = jnp.sum(b2_ref[t] * a2t, axis=0, keepdims=True)
        vt = [v_ref[t, vb * SUBLANES:(vb + 1) * SUBLANES, :] for vb in range(nvb)]
        for vb in range(nvb):
            y_ref[t, vb * SUBLANES:(vb + 1) * SUBLANES, :] = acc[nvb + vb] - skk[vb] * c1 + vt[vb] * c2

        def update_body(kc, c):
            for kq in range(k_unroll):
                k = kc * k_unroll + kq
                b1b = bcast(b1_ref, t, k)
                b2b = bcast(b2_ref, t, k)
                for vb in range(nvb):
                    idx = srow(k, vb)
                    s_ref[idx, :] = s_ref[idx, :] - skk[vb] * b1b + vt[vb] * b2b
            return c

        lax.fori_loop(0, nk // k_unroll, update_body, 0)
        return carry

    lax.fori_loop(0, tb, step, 0)

    def renorm_body(kc, c):
        for kq in range(k_unroll):
            k = kc * k_unroll + kq
            wb = bcast(wend_ref, 0, k)
            for vb in range(nvb):
                idx = srow(k, vb)
                s_ref[idx, :] = s_ref[idx, :] * wb
        return c

    lax.fori_loop(0, nk // k_unroll, renorm_body, 0)

    @pl.when(j == pl.num_programs(1) - 1)
    def _():
        sT_ref[...] = s_ref[...]


def _rwkv_scan(v, a1, a2, b1, b2, wend, s0):
    t, nk, lanes = v.shape
    tb = RWKV_BLOCK
    nblk = t // tb
    blk_idx = lambda d, j: j + d * (nblk - 1 - 2 * j)
    shared = pl.BlockSpec((tb, nk, lanes), lambda d, j: (blk_idx(d, j), 0, 0))
    perdir = pl.BlockSpec((None, tb, nk, lanes), lambda d, j: (d, blk_idx(d, j), 0, 0))
    ends = pl.BlockSpec((None, 1, nk, lanes), lambda d, j: (d, blk_idx(d, j), 0, 0))
    state = pl.BlockSpec((None, nk * nk, lanes), lambda d, j: (d, 0, 0))
    return pl.pallas_call(
        functools.partial(_rwkv_scan_kernel, k_unroll=16),
        grid=(2, nblk),
        in_specs=[shared, perdir, perdir, perdir, perdir, ends, state],
        out_specs=[perdir, state],
        out_shape=[jax.ShapeDtypeStruct((2, t, nk, lanes), F32), jax.ShapeDtypeStruct((2, nk * nk, lanes), F32)],
        scratch_shapes=[pltpu.VMEM((nk * nk, lanes), F32)],
        compiler_params=_cparams("arbitrary", "arbitrary"),
        name="rwkv_scan",
    )(v, a1, a2, b1, b2, wend, s0)


def _rwkv_out_kernel(yf_ref, yb_ref, bonus_ref, g_ref, gng_ref, gnb_ref, em_ref, o_ref):
    y = yf_ref[0, 0] + yb_ref[0, 0]
    em = em_ref[...]
    dlt = y - _sel_right(y, em)
    var = _sel_right(dlt * dlt, em)
    yn = dlt * lax.rsqrt(var + RWKV_GN_EPS) * gng_ref[...] + gnb_ref[...]
    o_ref[0] = (yn + bonus_ref[0]) * g_ref[0]


def _rwkv_out(y2, bonus, g, gn_g, gn_b, em, *, tt):
    _, bsz, t, wdt = y2.shape
    c2 = lambda b, j: (0, 0)
    tok = lambda b, j: (b, j, 0)
    return pl.pallas_call(
        _rwkv_out_kernel,
        grid=(bsz, t // tt),
        in_specs=[pl.BlockSpec((1, 1, tt, wdt), lambda b, j: (0, b, j, 0)),
                  pl.BlockSpec((1, 1, tt, wdt), lambda b, j: (1, b, j, 0)),
                  pl.BlockSpec((1, tt, wdt), tok), pl.BlockSpec((1, tt, wdt), tok),
                  pl.BlockSpec((1, wdt), c2), pl.BlockSpec((1, wdt), c2), pl.BlockSpec((wdt, wdt), c2)],
        out_specs=pl.BlockSpec((1, tt, wdt), tok),
        out_shape=jax.ShapeDtypeStruct((bsz, t, wdt), F32),
        compiler_params=_cparams("parallel", "parallel"),
        name="rwkv_out",
    )(y2, y2, bonus, g, gn_g, gn_b, em)


def _to_time_major(a):
    *lead, bsz, t, _ = a.shape
    nl = len(lead)
    a = a.reshape(*lead, bsz, t, RWKV_HEADS, RWKV_HEAD_DIM)
    a = a.transpose(*range(nl), nl + 1, nl + 3, nl, nl + 2)
    return a.reshape(*lead, t, RWKV_HEAD_DIM, bsz * RWKV_HEADS)


def _from_time_major(a, bsz):
    two, t, n, _ = a.shape
    a = a.reshape(two, t, n, bsz, RWKV_HEADS).transpose(0, 3, 1, 4, 2)
    return a.reshape(two, bsz, t, RWKV_HEADS * n)


def _na_kernel(q_ref, k_ref, v_ref, qc_ref, kc_ref, vc_ref, bias_ref, y_ref, yc_ref, kb_s, vb_s, kcb_s, vcb_s, *, rows):
    scale = NA_HEAD_DIM ** -0.5
    wq = GRID_W
    band = NA_WIN_R * GRID_W
    kb_s[...] = k_ref[0].astype(BF16)
    vb_s[...] = v_ref[0].astype(BF16)
    kcb_s[...] = kc_ref[0].astype(BF16)
    vcb_s[...] = vc_ref[0].astype(BF16)
    low = _iota((1, LANES), 1) < NA_HEAD_DIM

    def stack_heads(qs):
        return jnp.concatenate([jnp.where(low, qs, 0.0), jnp.where(low, 0.0, qs)], axis=0).astype(BF16)

    def softmax_parts(scores):
        m = functools.reduce(jnp.maximum, [jnp.max(s, axis=-1, keepdims=True) for s in scores])
        ps = [jnp.exp(s - m) for s in scores]
        l = functools.reduce(lambda a, b: a + b, [jnp.sum(p, axis=-1, keepdims=True) for p in ps])
        return ps, l

    def combine(ps, l, values):
        o = functools.reduce(lambda a, b: a + b, [_mm(p, v) for p, v in zip(ps, values)]) / l
        m = o.shape[0] // 2
        return jnp.where(low, o[0:m], o[m:2 * m])

    def row_group(i, c):
        jobs = []
        for u in range(NA_ROW_GROUP):
            r = i * NA_ROW_GROUP + u
            rs = jnp.clip(r - NA_WIN_R // 2, 0, rows - NA_WIN_R)
            var = jnp.where(r < NA_WIN_R // 2, r,
                            jnp.where(r > rows - NA_WIN_R // 2, r - (rows - NA_WIN_R), NA_WIN_R // 2))
            q0 = pl.multiple_of(r * wq, wq)
            k0 = pl.multiple_of(rs * wq, wq)
            qst = stack_heads(q_ref[0, pl.ds(q0, wq), :] * scale)
            scores = [_mm_nt(qst, kb_s[pl.ds(k0, band), :]) + bias_ref[0, var], _mm_nt(qst, kcb_s[...])]
            jobs.append((q0, k0, scores))
        soft = [softmax_parts(scores) for _, _, scores in jobs]
        for (q0, k0, _), (ps, l) in zip(jobs, soft):
            y_ref[0, pl.ds(q0, wq), :] = combine(ps, l, [vb_s[pl.ds(k0, band), :], vcb_s[...]])
        return c

    lax.fori_loop(0, rows // NA_ROW_GROUP, row_group, 0)
    ps, l = softmax_parts([_mm_nt(stack_heads(qc_ref[0] * scale), kcb_s[...])])
    yc_ref[0] = combine(ps, l, [vcb_s[...]])


def _na_bias_table(rpb, rows):
    cols = jnp.arange(GRID_W)
    col_start = jnp.clip(cols - NA_WIN_C // 2, 0, GRID_W - NA_WIN_C)
    col_in = (cols[None, :] >= col_start[:, None]) & (cols[None, :] < col_start[:, None] + NA_WIN_C)
    dc_idx = jnp.clip(cols[None, :] - cols[:, None] + NA_WIN_C - 1, 0, 2 * NA_WIN_C - 2)
    rpb_cols = jnp.where(col_in[None, None], rpb[:, :, dc_idx], -jnp.inf)
    half = NA_WIN_R // 2
    rep_rows = list(range(half)) + [half] + list(range(rows - half + 1, rows))
    tiles = []
    for r in rep_rows:
        rs = min(max(r - half, 0), rows - NA_WIN_R)
        dr_idx = rs + jnp.arange(NA_WIN_R) - r + NA_WIN_R - 1
        t = rpb_cols[:, dr_idx]
        tiles.append(t.transpose(0, 2, 1, 3).reshape(NA_HEADS, GRID_W, NA_WIN_R * GRID_W))
    tab = jnp.stack(tiles, axis=1)
    tab = tab.reshape(NA_HEADS // 2, 2, len(rep_rows), GRID_W, NA_WIN_R * GRID_W).transpose(0, 2, 1, 3, 4)
    return tab.reshape(NA_HEADS // 2, len(rep_rows), 2 * GRID_W, NA_WIN_R * GRID_W)


def _na(u, uc, bias_tab, col0):
    bsz, s, _ = u.shape
    lc = uc.shape[1]
    rows = s // GRID_W
    assert rows >= 2 * NA_WIN_R and rows % NA_ROW_GROUP == 0
    qb, kb, vb = col0 // LANES, col0 // LANES + 4, col0 // LANES + 8
    band = NA_WIN_R * GRID_W
    lat = lambda cb: pl.BlockSpec((1, s, LANES), lambda b, p: (b, 0, cb + p))
    cx = lambda cb: pl.BlockSpec((1, lc, LANES), lambda b, p: (b, 0, cb + p))
    return pl.pallas_call(
        functools.partial(_na_kernel, rows=rows),
        grid=(bsz, NA_HEADS // 2),
        in_specs=[lat(qb), lat(kb), lat(vb), cx(qb), cx(kb), cx(vb),
                  pl.BlockSpec((1, NA_WIN_R, 2 * GRID_W, band), lambda b, p: (p, 0, 0, 0))],
        out_specs=[pl.BlockSpec((1, s, LANES), lambda b, p: (b, 0, p)),
                   pl.BlockSpec((1, lc, LANES), lambda b, p: (b, 0, p))],
        out_shape=[jax.ShapeDtypeStruct((bsz, s, NA_HEADS * NA_HEAD_DIM), F32),
                   jax.ShapeDtypeStruct((bsz, lc, NA_HEADS * NA_HEAD_DIM), F32)],
        scratch_shapes=[pltpu.VMEM((s, LANES), BF16), pltpu.VMEM((s, LANES), BF16),
                        pltpu.VMEM((lc, LANES), BF16), pltpu.VMEM((lc, LANES), BF16)],
        compiler_params=_cparams("parallel", "arbitrary"),
        name="na",
    )(u, u, u, uc, uc, uc, bias_tab)


def _gla_kernel(q_ref, k_ref, v_ref, r_ref, gd_ref, kc_ref, vc_ref, gdc_ref, g2_ref, gb_ref, ng_ref, y_ref,
                bc_s, qt_s, kt_s, dec_s, ktc_s, decc_s, st_s, of_s):
    s_len = q_ref.shape[1]
    l_len = kc_ref.shape[1]
    cs = GLA_SUB
    scale = GLA_DK ** -0.5
    ri = _iota((LANES, LANES), 0)
    ci = _iota((LANES, LANES), 1)
    sub_shift = cs.bit_length() - 1
    same_chunk = (ri >> sub_shift) == (ci >> sub_shift)
    lane = _iota((1, LANES), 1)
    head_lane = [lane < GLA_DK, lane >= GLA_DK]
    st_mask = (_iota((2 * GLA_DV, LANES), 0) >= GLA_DV) == (_iota((2 * GLA_DV, LANES), 1) >= GLA_DK)
    trow = _iota((cs, 1), 0)

    for d in range(2):
        fwd = d == 0
        tri = same_chunk & ((ci <= ri) if fwd else (ci >= ri))
        cum_sel = jnp.concatenate([tri.astype(F32), same_chunk.astype(F32)], axis=0).astype(BF16)

        def gates(gd, d=d, cum_sel=cum_sel):
            g = _log_sigmoid(_mm(gd, g2_ref[d, 0]) + gb_ref[d, 0]) / GLA_GATE_NORM
            y = _sel_left(cum_sel, g)
            return y[0:LANES], y[LANES:2 * LANES]

        def prep_lat(i, c):
            rows = pl.ds(pl.multiple_of(i * LANES, LANES), LANES)
            bc, tot = gates(gd_ref[0, rows, :])
            bc_s[rows, :] = bc
            qt_s[rows, :] = q_ref[0, rows, :] * scale * jnp.exp(bc)
            kt_s[rows, :] = k_ref[0, rows, :] * jnp.exp(tot - bc)
            dec_s[rows, :] = jnp.exp(tot)
            return c

        def prep_ctx(i, c):
            rows = pl.ds(pl.multiple_of(i * LANES, LANES), LANES)
            bc, tot = gates(gdc_ref[0, rows, :])
            ktc_s[rows, :] = kc_ref[0, rows, :] * jnp.exp(tot - bc)
            decc_s[rows, :] = jnp.exp(tot)
            return c

        lax.fori_loop(0, s_len // LANES, prep_lat, 0, unroll=2)
        lax.fori_loop(0, l_len // LANES, prep_ctx, 0, unroll=2)
        st_s[...] = jnp.zeros(st_s.shape, F32)

        grp = GLA_GROUP
        order = tuple(range(grp)) if fwd else tuple(range(grp - 1, -1, -1))

        def group_rows(i, n_groups, fwd=fwd):
            g = i if fwd else n_groups - 1 - i
            return [pl.ds(pl.multiple_of((g * grp + u) * cs, cs), cs) for u in range(grp)]

        def increment(kt, vv):
            return jnp.where(st_mask, _mm_tn(vv, kt), 0.0)

        def ctx_group(i, c, order=order):
            rows = group_rows(i, l_len // (cs * grp))
            incs = [increment(ktc_s[rows[u], :], vc_ref[0, rows[u], :]) for u in range(grp)]
            st = st_s[...]
            for u in order:
                st = st * decc_s[rows[u], :][0:1, :] + incs[u]
            st_s[...] = st
            return c

        def lat_group(i, c, fwd=fwd, order=order):
            rows = group_rows(i, s_len // (cs * grp))
            vvs = [v_ref[0, rows[u], :] for u in range(grp)]
            incs = [increment(kt_s[rows[u], :], vvs[u]) for u in range(grp)]
            st = st_s[...]
            inter = [None] * grp
            for u in order:
                inter[u] = _mm_nt(qt_s[rows[u], :], st)
                st = st * dec_s[rows[u], :][0:1, :] + incs[u]
            st_s[...] = st
            for u in range(grp):
                vv = vvs[u]
                bc = bc_s[rows[u], :]
                qs = q_ref[0, rows[u], :] * scale
                kk = k_ref[0, rows[u], :]
                o0, o1 = inter[u][:, 0:GLA_DV], inter[u][:, GLA_DV:2 * GLA_DV]
                for s in range(cs):
                    keep = (trow >= s) if fwd else (trow <= s)
                    e = jnp.exp(jnp.where(keep, bc - bc[s:s + 1, :], -jnp.inf))
                    term = qs * kk[s:s + 1, :] * e
                    a0 = jnp.sum(jnp.where(head_lane[0], term, 0.0), axis=-1, keepdims=True)
                    a1 = jnp.sum(jnp.where(head_lane[1], term, 0.0), axis=-1, keepdims=True)
                    o0 = o0 + a0 * vv[s:s + 1, 0:GLA_DV]
                    o1 = o1 + a1 * vv[s:s + 1, GLA_DV:2 * GLA_DV]
                if fwd:
                    of_s[rows[u], 0:GLA_DV] = o0
                    of_s[rows[u], GLA_DV:2 * GLA_DV] = o1
                else:
                    rr = r_ref[0, rows[u], :]
                    for h, oh in enumerate((o0, o1)):
                        cols = slice(h * GLA_DV, (h + 1) * GLA_DV)
                        ot = of_s[rows[u], cols] + oh
                        on = ot * lax.rsqrt(jnp.mean(ot * ot, axis=-1, keepdims=True) + RMS_EPS) * ng_ref[...]
                        rh = rr[:, cols]
                        y_ref[0, rows[u], cols] = on * (rh * _sigmoid(rh))
            return c

        lax.fori_loop(0, l_len // (cs * grp), ctx_group, 0)
        lax.fori_loop(0, s_len // (cs * grp), lat_group, 0)


def _gla(u, uc, g2p, gbp, norm_g, gd_block):
    bsz, s, _ = u.shape
    lc = uc.shape[1]
    pairs = GLA_HEADS // 2
    dv2 = 2 * GLA_DV
    lat = lambda width, blk: pl.BlockSpec((1, s, width), lambda b, p: (b, 0, blk(p)))
    cx = lambda width, blk: pl.BlockSpec((1, lc, width), lambda b, p: (b, 0, blk(p)))
    q_blk = lambda p: p
    k_blk = lambda p: pairs + p
    v_blk = lambda p: 2 * pairs * LANES // dv2 + p
    r_blk = lambda p: (2 * pairs * LANES + GLA_HEADS * GLA_DV) // dv2 + p
    gd_blk = lambda p: gd_block
    return pl.pallas_call(
        _gla_kernel,
        grid=(bsz, pairs),
        in_specs=[lat(LANES, q_blk), lat(LANES, k_blk), lat(dv2, v_blk), lat(dv2, r_blk), lat(LANES, gd_blk),
                  cx(LANES, k_blk), cx(dv2, v_blk), cx(LANES, gd_blk),
                  pl.BlockSpec((2, 1, LANES, LANES), lambda b, p: (0, p, 0, 0)),
                  pl.BlockSpec((2, 1, 1, LANES), lambda b, p: (0, p, 0, 0)),
                  pl.BlockSpec((1, GLA_DV), lambda b, p: (0, 0))],
        out_specs=pl.BlockSpec((1, s, dv2), lambda b, p: (b, 0, p)),
        out_shape=jax.ShapeDtypeStruct((bsz, s, GLA_HEADS * GLA_DV), F32),
        scratch_shapes=[pltpu.VMEM((s, LANES), F32)] * 4 + [pltpu.VMEM((lc, LANES), F32)] * 2
                       + [pltpu.VMEM((dv2, LANES), F32), pltpu.VMEM((s, dv2), F32)],
        compiler_params=_cparams("parallel", "arbitrary"),
        name="gla",
    )(u, u, u, u, u, uc, uc, uc, g2p, gbp, norm_g)


def _diff_kernel(q_ref, k_ref, v_ref, kc_ref, vc_ref, lam_ref, ng_ref, y_ref, k_s, v_s, *, lambda_init):
    scale = DIFF_DH ** -0.5
    s_len = k_ref.shape[1]

    @pl.when(pl.program_id(2) == 0)
    def _():
        k_s[0:s_len, :] = k_ref[0].astype(BF16)
        v_s[0:s_len, :] = v_ref[0].astype(BF16)
        k_s[s_len:, :] = kc_ref[0].astype(BF16)
        v_s[s_len:, :] = vc_ref[0].astype(BF16)

    lp = lam_ref[...]
    lam = (jnp.exp(jnp.sum(lp[0:1] * lp[1:2], axis=-1, keepdims=True))
           - jnp.exp(jnp.sum(lp[2:3] * lp[3:4], axis=-1, keepdims=True)) + lambda_init)
    qs = q_ref[0] * scale
    low = _iota((1, LANES), 1) < DIFF_DH
    outs = []
    for m in range(2):
        qm = jnp.where(low, qs, 0.0) if m == 0 else jnp.where(low, 0.0, qs)
        s = _mm_nt(qm, k_s[...])
        p = jnp.exp(s - jnp.max(s, axis=-1, keepdims=True))
        outs.append(_mm(p, v_s[...]) / jnp.sum(p, axis=-1, keepdims=True))
    o = outs[0] - lam * outs[1]
    on = o * lax.rsqrt(jnp.mean(o * o, axis=-1, keepdims=True) + RMS_EPS) * ng_ref[...]
    y_ref[0] = on * (1.0 - lambda_init)


def _diff(u, uc, lam_params, norm_g, col0, lambda_init, *, tq):
    bsz, s, _ = u.shape
    lc = uc.shape[1]
    qb = col0 // LANES
    kb, vb = qb + DIFF_HEADS, qb + 2 * DIFF_HEADS
    full = lambda arr_len, cb: pl.BlockSpec((1, arr_len, LANES), lambda b, h, j: (b, 0, cb + h))
    return pl.pallas_call(
        functools.partial(_diff_kernel, lambda_init=lambda_init),
        grid=(bsz, DIFF_HEADS, s // tq),
        in_specs=[pl.BlockSpec((1, tq, LANES), lambda b, h, j: (b, j, qb + h)),
                  full(s, kb), full(s, vb), full(lc, kb), full(lc, vb),
                  pl.BlockSpec((4, DIFF_DH), lambda b, h, j: (0, 0)),
                  pl.BlockSpec((1, DIFF_DV), lambda b, h, j: (0, 0))],
        out_specs=pl.BlockSpec((1, tq, LANES), lambda b, h, j: (b, j, h)),
        out_shape=jax.ShapeDtypeStruct((bsz, s, DIFF_HEADS * DIFF_DV), F32),
        scratch_shapes=[pltpu.VMEM((s + lc, LANES), BF16), pltpu.VMEM((s + lc, LANES), BF16)],
        compiler_params=_cparams("parallel", "parallel", "arbitrary"),
        name="diff_attn",
    )(u, u, u, uc, uc, lam_params, norm_g)


def _head_selector(width, head_dim, value):
    idx = jnp.arange(width) // head_dim
    return jnp.where(idx[:, None] == idx[None, :], value, 0.0).astype(F32)


def _pad_rows(w, lo, total):
    return jnp.zeros((total, w.shape[1]), w.dtype).at[lo:lo + w.shape[0]].set(w)


def _rope_tables(s, reps):
    pos = jnp.arange(s)
    n = DIFF_DH // 4
    freqs = ROPE_BASE ** (-jnp.arange(n, dtype=F32) / n)
    ang_r = (pos // GRID_W).astype(F32)[:, None] * freqs[None, :]
    ang_c = (pos % GRID_W).astype(F32)[:, None] * freqs[None, :]
    cos = jnp.concatenate([jnp.cos(ang_r)] * 2 + [jnp.cos(ang_c)] * 2, axis=-1)
    sin = jnp.concatenate([-jnp.sin(ang_r), jnp.sin(ang_r), -jnp.sin(ang_c), jnp.sin(ang_c)], axis=-1)
    return jnp.tile(cos, (1, reps)), jnp.tile(sin, (1, reps))


def _rwkv_na_mixer(xl, xc, mods, ctx_row, w_in, mu, w0, w2, a0, a2, g2, k_k, k_a, r_k, gn_g, gn_b, rpb):
    bsz, s, _ = xl.shape
    lc = xc.shape[1]
    n = w_in.shape[1]
    chunks = tuple((lo, min(lo + 512, n)) for lo in range(0, n, 512))
    u = _proj(xl, mods, w_in, chunks, tm=512)
    uc = _proj(xc, mods, w_in, chunks, tm=lc, ctx_row=ctx_row)

    eh = _head_selector(RWKV_WIDTH, RWKV_HEAD_DIM, 1.0).astype(BF16)
    em = _head_selector(RWKV_WIDTH, RWKV_HEAD_DIM, 1.0 / RWKV_HEAD_DIM).astype(BF16)
    p = dict(mu=mu[None], w0=w0, a0=a0, g2=g2.astype(BF16), k_k=k_k[None], k_a=k_a[None], r_k=r_k.reshape(1, -1), eh=eh,
             w2=jnp.stack([_pad_rows(w2[0], 0, 128), _pad_rows(w2[1], 64, 128)]).astype(BF16),
             a2=jnp.stack([_pad_rows(a2[0], 0, 128), _pad_rows(a2[1], 64, 128)]).astype(BF16))
    lanes = bsz * RWKV_HEADS
    state = jnp.zeros((2, RWKV_HEAD_DIM * RWKV_HEAD_DIM, lanes), F32)
    ys = []
    for uu, tt in ((uc, lc), (u, 256)):
        v, a1, a2, b1, b2, wend, bonus, g = _rwkv_prep(uu, p, tt=tt)
        y2, state = _rwkv_scan(_to_time_major(v), _to_time_major(a1), _to_time_major(a2), _to_time_major(b1),
                               _to_time_major(b2), _to_time_major(wend), state)
        ys.append(_rwkv_out(_from_time_major(y2, bsz), bonus, g, gn_g[None], gn_b[None], em, tt=tt))
    yc_a, y_a = ys
    y_b, yc_b = _na(u, uc, _na_bias_table(rpb, s // GRID_W), RWKV_IN)
    return (y_a, y_b), (yc_a, yc_b)


def _gla_diff_mixer(xl, xc, mods, ctx_row, w_in, gla_g2, gla_gb, gla_norm_g, diff_lambda, diff_norm_g, lambda_init):
    bsz, s, d = xl.shape
    lc = xc.shape[1]
    n_gla = 2 * GLA_HEADS * GLA_DK + 2 * GLA_HEADS * GLA_DV
    n_gate = 2 * GLA_GATE_LORA
    n_diff = 2 * DIFF_HEADS * 2 * DIFF_DH + DIFF_HEADS * DIFF_DV
    w_perm = jnp.concatenate([w_in[:, :n_gla], w_in[:, n_gla + n_gate:n_gla + n_gate + n_diff],
                              w_in[:, n_gla:n_gla + n_gate], jnp.zeros((d, LANES - n_gate), w_in.dtype)], axis=1).astype(BF16)
    rope_w = 2 * DIFF_HEADS * 2 * DIFF_DH
    chunks = ((0, 512), (512, 1024), (1024, n_gla), (n_gla, n_gla + rope_w),
              (n_gla + rope_w, n_gla + n_diff), (n_gla + n_diff, n_gla + n_diff + LANES))
    cos_t, sin_t = _rope_tables(s, rope_w // DIFF_DH)
    u = _proj(xl, mods, w_perm, chunks, tm=512, rope=(3, cos_t, sin_t))
    uc = _proj(xc, mods, w_perm, chunks, tm=lc, ctx_row=ctx_row)

    pairs = GLA_HEADS // 2
    g2p = jnp.stack([jnp.stack([_pad_rows(gla_g2[dd][:, p * LANES:(p + 1) * LANES], dd * GLA_GATE_LORA, LANES)
                                for p in range(pairs)]) for dd in range(2)]).astype(BF16)
    gbp = gla_gb.reshape(2, pairs, 1, LANES)
    y_c = _gla(u, uc, g2p, gbp, gla_norm_g[None], (n_gla + n_diff) // LANES)
    y_d = _diff(u, uc, diff_lambda, diff_norm_g[None], n_gla, lambda_init, tq=512)
    return y_c, y_d


def kernel(x, c, ctx, c_ctx, w_mod_0, b_mod_0, ln_g_0, ln_b_0, ffn_gu_0, ffn_down_0, w_in_0, w_out_0, rwkv_mu_0, rwkv_w0_0, rwkv_w2_0, rwkv_a0_0, rwkv_a2_0, rwkv_g2_0, rwkv_k_k_0, rwkv_k_a_0, rwkv_r_k_0, rwkv_gn_g_0, rwkv_gn_b_0, na_rpb_0, w_mod_1, b_mod_1, ln_g_1, ln_b_1, ffn_gu_1, ffn_down_1, w_in_1, w_out_1, gla_g2_1, gla_gb_1, gla_norm_g_1, diff_lambda_1, diff_norm_g_1):
    bsz, s, d = x.shape
    lc = ctx.shape[1]
    ctx_row = bsz
    rows = -(-(bsz + 1) // SUBLANES) * SUBLANES
    c_all = jnp.concatenate([c, c_ctx[None], jnp.zeros((rows - bsz - 1, d), F32)], axis=0)
    tm = 512

    mods = _mods(c_all, w_mod_0, b_mod_0)
    gu, dn = ffn_gu_0.astype(BF16), ffn_down_0.astype(BF16)
    xl = _ffn(x, mods, 0, gu[0], dn[0], ln_g_0[0], ln_b_0[0], tm=tm)
    xc = _ffn(ctx, mods, 0, gu[0], dn[0], ln_g_0[0], ln_b_0[0], tm=lc, ctx_row=ctx_row)
    (y_a, y_b), (yc_a, yc_b) = _rwkv_na_mixer(xl, xc, mods, ctx_row, w_in_0.astype(BF16), rwkv_mu_0, rwkv_w0_0, rwkv_w2_0,
                                              rwkv_a0_0, rwkv_a2_0, rwkv_g2_0, rwkv_k_k_0, rwkv_k_a_0, rwkv_r_k_0,
                                              rwkv_gn_g_0, rwkv_gn_b_0, na_rpb_0)
    wo = w_out_0.astype(BF16)
    xl = _mixout(xl, mods, y_a, y_b, wo, ln_g_0[1], ln_b_0[1], tm=tm)
    xc = _mixout(xc, mods, yc_a, yc_b, wo, ln_g_0[1], ln_b_0[1], tm=lc, ctx_row=ctx_row)
    xl = _ffn(xl, mods, 6, gu[1], dn[1], ln_g_0[2], ln_b_0[2], tm=tm)
    xc = _ffn(xc, mods, 6, gu[1], dn[1], ln_g_0[2], ln_b_0[2], tm=lc, ctx_row=ctx_row)

    mods = _mods(c_all, w_mod_1, b_mod_1)
    gu, dn = ffn_gu_1.astype(BF16), ffn_down_1.astype(BF16)
    xl = _ffn(xl, mods, 0, gu[0], dn[0], ln_g_1[0], ln_b_1[0], tm=tm)
    xc = _ffn(xc, mods, 0, gu[0], dn[0], ln_g_1[0], ln_b_1[0], tm=lc, ctx_row=ctx_row)
    y_c, y_d = _gla_diff_mixer(xl, xc, mods, ctx_row, w_in_1, gla_g2_1, gla_gb_1, gla_norm_g_1, diff_lambda_1, diff_norm_g_1,
                               0.8 - 0.6 * math.exp(-0.3 * 1))
    xl = _mixout(xl, mods, y_c, y_d, w_out_1.astype(BF16), ln_g_1[1], ln_b_1[1], tm=tm)
    xl = _ffn(xl, mods, 6, gu[1], dn[1], ln_g_1[2], ln_b_1[2], tm=tm)
    return xl
```

```python
import functools
import math

import jax
import jax.numpy as jnp
from jax import lax
from jax.experimental import pallas as pl
from jax.experimental.pallas import tpu as pltpu

F32 = jnp.float32
BF16 = jnp.bfloat16

GRID_W = 64
N_MOD = 9
DEPTH = 2
LN_EPS = 1e-5
RMS_EPS = 1e-6
DEEPNORM_ALPHA = (2 * DEPTH) ** 0.25
RWKV_HEADS = 8
RWKV_HEAD_DIM = 64
RWKV_WIDTH = RWKV_HEADS * RWKV_HEAD_DIM
RWKV_IN = 3 * RWKV_WIDTH + 4 * 64 + 128
RWKV_GN_EPS = 64e-5
RWKV_BLOCK = 16
NA_HEADS = 8
NA_HEAD_DIM = 64
NA_WIN_R = 8
NA_WIN_C = 16
NA_ROW_GROUP = 4
GLA_HEADS = 4
GLA_DK = 64
GLA_DV = 128
GLA_GATE_LORA = 16
GLA_GATE_NORM = 16.0
GLA_SUB = 16
GLA_GROUP = 8
DIFF_HEADS = 4
DIFF_DH = 64
DIFF_DV = 128
ROPE_BASE = 10000.0

LANES = 128
SUBLANES = 8
V7X_VMEM_BYTES = 64 * 1024 * 1024
VMEM_LIMIT = (V7X_VMEM_BYTES * 7) // 8


def _cparams(*sem):
    return pltpu.CompilerParams(dimension_semantics=sem, vmem_limit_bytes=VMEM_LIMIT)


def _mm(a, b):
    return jnp.dot(a.astype(BF16), b.astype(BF16), preferred_element_type=F32)


def _mm_nt(a, b):
    return lax.dot_general(a.astype(BF16), b.astype(BF16), (((1,), (1,)), ((), ())), preferred_element_type=F32)


def _mm_tn(a, b):
    return lax.dot_general(a.astype(BF16), b.astype(BF16), (((0,), (0,)), ((), ())), preferred_element_type=F32)


def _split3(x):
    hi = x.astype(BF16)
    r1 = x - hi.astype(F32)
    mid = r1.astype(BF16)
    lo = (r1 - mid.astype(F32)).astype(BF16)
    return hi, mid, lo


def _sel_right(x, sel):
    m = x.shape[0]
    y = jnp.dot(jnp.concatenate(_split3(x), axis=0), sel, preferred_element_type=F32)
    return y[0:m] + y[m:2 * m] + y[2 * m:3 * m]


def _sel_left(sel, x):
    n = x.shape[1]
    y = jnp.dot(sel, jnp.concatenate(_split3(x), axis=1), preferred_element_type=F32)
    return y[:, 0:n] + y[:, n:2 * n] + y[:, 2 * n:3 * n]


def _sigmoid(x):
    return 1.0 / (1.0 + jnp.exp(-x))


def _log_sigmoid(x):
    return -(jnp.maximum(-x, 0.0) + jnp.log1p(jnp.exp(-jnp.abs(x))))


def _layer_norm(z, g, b):
    mu = jnp.mean(z, axis=-1, keepdims=True)
    d = z - mu
    var = jnp.mean(d * d, axis=-1, keepdims=True)
    return d * lax.rsqrt(var + LN_EPS) * g + b


def _iota(shape, dim):
    return lax.broadcasted_iota(jnp.int32, shape, dim)


def _mods_kernel(c_ref, w_ref, b_ref, o_ref):
    c = c_ref[...]
    o_ref[...] = _mm(c * _sigmoid(c), w_ref[...]) + b_ref[...]


def _mods(c_all, w_mod, b_mod):
    rows, d = c_all.shape
    n = w_mod.shape[1]
    tn = n // 8
    out = pl.pallas_call(
        _mods_kernel,
        grid=(n // tn,),
        in_specs=[pl.BlockSpec((rows, d), lambda j: (0, 0)),
                  pl.BlockSpec((d, tn), lambda j: (0, j)),
                  pl.BlockSpec((1, tn), lambda j: (0, j))],
        out_specs=pl.BlockSpec((rows, tn), lambda j: (0, j)),
        out_shape=jax.ShapeDtypeStruct((rows, n), F32),
        compiler_params=_cparams("arbitrary"),
        name="mods",
    )(c_all, w_mod, b_mod[None])
    return out.reshape(rows, N_MOD, d)


def _mod_spec(d, ctx_row):
    if ctx_row is None:
        return pl.BlockSpec((1, N_MOD, d), lambda b, j: (b, 0, 0))
    return pl.BlockSpec((1, N_MOD, d), lambda b, j: (ctx_row, 0, 0))


def _ffn_kernel(x_ref, m_ref, wgu_ref, wd_ref, g_ref, b_ref, o_ref, *, i0, fc):
    hidden = wd_ref.shape[0]
    x = x_ref[0]
    shift, scale, gate = (m_ref[0, i0 + i:i0 + i + 1, :] for i in range(3))
    h = (x * (1.0 + scale) + shift).astype(BF16)
    acc = None
    for c in range(hidden // fc):
        g = jnp.dot(h, wgu_ref[:, c * fc:(c + 1) * fc], preferred_element_type=F32)
        u = jnp.dot(h, wgu_ref[:, hidden + c * fc:hidden + (c + 1) * fc], preferred_element_type=F32)
        a = (g * _sigmoid(g) * u).astype(BF16)
        dn = jnp.dot(a, wd_ref[c * fc:(c + 1) * fc, :], preferred_element_type=F32)
        acc = dn if acc is None else acc + dn
    z = DEEPNORM_ALPHA * x + gate * (0.5 * acc)
    o_ref[0] = _layer_norm(z, g_ref[...], b_ref[...])


def _ffn(x, mods, i0, w_gu, w_down, ln_g, ln_b, *, tm, ctx_row=None):
    bsz, t, d = x.shape
    hidden = w_down.shape[0]
    const = lambda b, j: (0, 0)
    return pl.pallas_call(
        functools.partial(_ffn_kernel, i0=i0, fc=256),
        grid=(bsz, t // tm),
        in_specs=[pl.BlockSpec((1, tm, d), lambda b, j: (b, j, 0)),
                  _mod_spec(d, ctx_row),
                  pl.BlockSpec((d, 2 * hidden), const, pipeline_mode=pl.Buffered(1)),
                  pl.BlockSpec((hidden, d), const, pipeline_mode=pl.Buffered(1)),
                  pl.BlockSpec((1, d), const),
                  pl.BlockSpec((1, d), const)],
        out_specs=pl.BlockSpec((1, tm, d), lambda b, j: (b, j, 0)),
        out_shape=jax.ShapeDtypeStruct(x.shape, F32),
        compiler_params=_cparams("parallel", "parallel"),
        name="ffn",
    )(x, mods, w_gu, w_down, ln_g[None], ln_b[None])


def _proj_kernel(x_ref, m_ref, w_ref, *rest, chunks, rope_chunk):
    if rope_chunk is None:
        (o_ref,) = rest
    else:
        cos_ref, sin_ref, o_ref = rest
    x = x_ref[0]
    h = (x * (1.0 + m_ref[0, 4:5, :]) + m_ref[0, 3:4, :]).astype(BF16)
    for ci, (lo, hi) in enumerate(chunks):
        u = jnp.dot(h, w_ref[:, lo:hi], preferred_element_type=F32)
        if ci == rope_chunk:
            n = hi - lo
            first = (_iota((1, n), 1) & 16) == 0
            partner = jnp.where(first, pltpu.roll(u, n - 16, axis=1), pltpu.roll(u, 16, axis=1))
            u = u * cos_ref[...] + partner * sin_ref[...]
        o_ref[0, :, lo:hi] = u


def _proj(x, mods, w_in, chunks, *, tm, ctx_row=None, rope=None):
    bsz, t, d = x.shape
    n = w_in.shape[1]
    const = lambda b, j: (0, 0)
    in_specs = [pl.BlockSpec((1, tm, d), lambda b, j: (b, j, 0)),
                _mod_spec(d, ctx_row),
                pl.BlockSpec((d, n), const, pipeline_mode=pl.Buffered(1))]
    args = [x, mods, w_in]
    rope_chunk = None
    if rope is not None:
        rope_chunk, cos_t, sin_t = rope
        width = cos_t.shape[1]
        in_specs += [pl.BlockSpec((tm, width), lambda b, j: (j, 0))] * 2
        args += [cos_t, sin_t]
    return pl.pallas_call(
        functools.partial(_proj_kernel, chunks=chunks, rope_chunk=rope_chunk),
        grid=(bsz, t // tm),
        in_specs=in_specs,
        out_specs=pl.BlockSpec((1, tm, n), lambda b, j: (b, j, 0)),
        out_shape=jax.ShapeDtypeStruct((bsz, t, n), F32),
        compiler_params=_cparams("parallel", "parallel"),
        name="proj",
    )(*args)


def _mixout_kernel(x_ref, m_ref, ya_ref, yb_ref, w_ref, g_ref, b_ref, o_ref):
    x = x_ref[0]
    half = ya_ref.shape[2]
    y = _mm(ya_ref[0], w_ref[0:half, :]) + _mm(yb_ref[0], w_ref[half:2 * half, :])
    z = DEEPNORM_ALPHA * x + m_ref[0, 5:6, :] * y
    o_ref[0] = _layer_norm(z, g_ref[...], b_ref[...])


def _mixout(x, mods, ya, yb, w_out, ln_g, ln_b, *, tm, ctx_row=None):
    bsz, t, d = x.shape
    half = ya.shape[2]
    const = lambda b, j: (0, 0)
    tok = lambda b, j: (b, j, 0)
    return pl.pallas_call(
        _mixout_kernel,
        grid=(bsz, t // tm),
        in_specs=[pl.BlockSpec((1, tm, d), tok), _mod_spec(d, ctx_row),
                  pl.BlockSpec((1, tm, half), tok), pl.BlockSpec((1, tm, half), tok),
                  pl.BlockSpec((2 * half, d), const, pipeline_mode=pl.Buffered(1)),
                  pl.BlockSpec((1, d), const), pl.BlockSpec((1, d), const)],
        out_specs=pl.BlockSpec((1, tm, d), tok),
        out_shape=jax.ShapeDtypeStruct(x.shape, F32),
        compiler_params=_cparams("parallel", "parallel"),
        name="mixout",
    )(x, mods, ya, yb, w_out, ln_g[None], ln_b[None])


def _rwkv_prep_kernel(u_ref, up_ref, un_ref, mu_ref, w0_ref, w2_ref, a0_ref, a2_ref, g2_ref, kk_ref, ka_ref, rk_ref,
                      eh_ref, cum_ref, v_o, a1_o, a2_o, b1_o, b2_o, wend_o, bonus_o, g_o, wfull_s, *, tt):
    j = pl.program_id(1)
    nj = pl.num_programs(1)
    wdt = RWKV_WIDTH
    u = u_ref[0]
    prev_row = up_ref[0, SUBLANES - 1:SUBLANES, :] * (j > 0).astype(F32)
    next_row = un_ref[0, 0:1, :] * (j < nj - 1).astype(F32)
    row = _iota((tt, 1), 0)
    up = jnp.where(row == 0, prev_row, pltpu.roll(u, 1, axis=0))
    un = jnp.where(row == tt - 1, next_row, pltpu.roll(u, tt - 1, axis=0))
    us = u + mu_ref[...] * (0.5 * (up + un) - u)
    r, k, v = us[:, 0:wdt], us[:, wdt:2 * wdt], us[:, 2 * wdt:3 * wdt]
    wd = jnp.tanh(us[:, 3 * wdt:3 * wdt + 128])
    ad = us[:, 3 * wdt + 128:3 * wdt + 256]
    gd = us[:, 3 * wdt + 256:3 * wdt + 384]
    eh = eh_ref[...]
    kkn = k * kk_ref[...]
    kkn = kkn / jnp.maximum(jnp.sqrt(_sel_right(kkn * kkn, eh)), 1e-12)
    bonus_o[0] = _sel_right(r * k * rk_ref[...], eh) * v
    g_o[0] = _mm(_sigmoid(gd), g2_ref[...])
    v_o[0] = v
    blk = RWKV_BLOCK
    for d in range(2):
        wl = w0_ref[d:d + 1, :] + _mm(wd, w2_ref[d])
        e = jnp.exp(_log_sigmoid(wl) - 0.5)
        e_inc = _sel_left(cum_ref[d], e)
        w_inc = jnp.exp(-e_inc)
        inv_w = jnp.exp(e_inc)
        a = _sigmoid(a0_ref[d:d + 1, :] + _mm(ad, a2_ref[d]))
        a1_o[d, 0] = kkn * jnp.exp(e - e_inc)
        a2_o[d, 0] = r * w_inc
        b1_o[d, 0] = kkn * a * inv_w
        b2_o[d, 0] = k * (1.0 + (a - 1.0) * ka_ref[...]) * inv_w
        last = blk - 1 if d == 0 else 0
        for c in range(wdt // LANES):
            cols = slice(c * LANES, (c + 1) * LANES)
            wfull_s[c] = w_inc[:, cols]
            wend_o[d, 0, :, cols] = wfull_s[c, pl.ds(last, tt // blk, stride=blk), :]


def _rwkv_prep(u, p, *, tt):
    bsz, t, _ = u.shape
    wdt = RWKV_WIDTH
    nblk8 = t // SUBLANES
    per = tt // SUBLANES
    c2 = lambda b, j: (0, 0)
    c3 = lambda b, j: (0, 0, 0)
    tok = lambda b, j: (b, j, 0)
    tok2 = lambda b, j: (0, b, j, 0)
    one = jax.ShapeDtypeStruct((bsz, t, wdt), F32)
    two = jax.ShapeDtypeStruct((2, bsz, t, wdt), F32)
    blk = RWKV_BLOCK
    ends = jax.ShapeDtypeStruct((2, bsz, t // blk, wdt), F32)
    ri = jnp.arange(tt)[:, None]
    ci = jnp.arange(tt)[None, :]
    same = (ri // blk) == (ci // blk)
    cum_sel = jnp.stack([same & (ci <= ri), same & (ci >= ri)]).astype(BF16)
    return pl.pallas_call(
        functools.partial(_rwkv_prep_kernel, tt=tt),
        grid=(bsz, t // tt),
        in_specs=[pl.BlockSpec((1, tt, RWKV_IN), tok),
                  pl.BlockSpec((1, SUBLANES, RWKV_IN), lambda b, j: (b, jnp.maximum(j * per - 1, 0), 0)),
                  pl.BlockSpec((1, SUBLANES, RWKV_IN), lambda b, j: (b, jnp.minimum((j + 1) * per, nblk8 - 1), 0)),
                  pl.BlockSpec((1, RWKV_IN), c2),
                  pl.BlockSpec((2, wdt), c2), pl.BlockSpec((2, 128, wdt), c3),
                  pl.BlockSpec((2, wdt), c2), pl.BlockSpec((2, 128, wdt), c3),
                  pl.BlockSpec((128, wdt), c2),
                  pl.BlockSpec((1, wdt), c2), pl.BlockSpec((1, wdt), c2), pl.BlockSpec((1, wdt), c2),
                  pl.BlockSpec((wdt, wdt), c2), pl.BlockSpec((2, tt, tt), c3)],
        out_specs=[pl.BlockSpec((1, tt, wdt), tok)] + [pl.BlockSpec((2, 1, tt, wdt), tok2)] * 4
                  + [pl.BlockSpec((2, 1, tt // blk, wdt), tok2)] + [pl.BlockSpec((1, tt, wdt), tok)] * 2,
        out_shape=[one, two, two, two, two, ends, one, one],
        scratch_shapes=[pltpu.VMEM((wdt // LANES, tt, LANES), F32)],
        compiler_params=_cparams("parallel", "parallel"),
        name="rwkv_prep",
    )(u, u, u, p["mu"], p["w0"], p["w2"], p["a0"], p["a2"], p["g2"], p["k_k"], p["k_a"], p["r_k"], p["eh"], cum_sel)


def _rwkv_scan_kernel(v_ref, a1_ref, a2_ref, b1_ref, b2_ref, wend_ref, s0_ref, y_ref, sT_ref, s_ref, *, k_unroll):
    d = pl.program_id(0)
    j = pl.program_id(1)
    tb = RWKV_BLOCK
    nk = RWKV_HEAD_DIM
    nvb = RWKV_HEAD_DIM // SUBLANES
    lanes = s_ref.shape[1]

    @pl.when(j == 0)
    def _():
        s_ref[...] = s0_ref[...]

    def bcast(ref, t, k):
        return jnp.broadcast_to(ref[t, pl.ds(k, 1), :], (SUBLANES, lanes))

    def srow(k, vb):
        return pl.ds(pl.multiple_of(k * nk + vb * SUBLANES, SUBLANES), SUBLANES)

    def step(i, carry):
        t = jnp.where(d == 0, i, tb - 1 - i)

        def reduce_body(kc, acc):
            acc = list(acc)
            for kq in range(k_unroll):
                k = kc * k_unroll + kq
                a1b = bcast(a1_ref, t, k)
                a2b = bcast(a2_ref, t, k)
                for vb in range(nvb):
                    s = s_ref[srow(k, vb), :]
                    acc[vb] = acc[vb] + s * a1b
                    acc[nvb + vb] = acc[nvb + vb] + s * a2b
            return tuple(acc)

        zero = jnp.zeros((SUBLANES, lanes), F32)
        acc = lax.fori_loop(0, nk // k_unroll, reduce_body, (zero,) * (2 * nvb))
        skk = acc[:nvb]
        a2t = a2_ref[t]
        c1 = jnp.sum(b1_ref[t] * a2t, axis=0, keepdims=True)
        c2 = jnp.sum(b2_ref[t] * a2t, axis=0, keepdims=True)
        vt = [v_ref[t, vb * SUBLANES:(vb + 1) * SUBLANES, :] for vb in range(nvb)]
        for vb in range(nvb):
            y_ref[t, vb * SUBLANES:(vb + 1) * SUBLANES, :] = acc[nvb + vb] - skk[vb] * c1 + vt[vb] * c2

        def update_body(kc, c):
            for kq in range(k_unroll):
                k = kc * k_unroll + kq
                b1b = bcast(b1_ref, t, k)
                b2b = bcast(b2_ref, t, k)
                for vb in range(nvb):
                    idx = srow(k, vb)
                    s_ref[idx, :] = s_ref[idx, :] - skk[vb] * b1b + vt[vb] * b2b
            return c

        lax.fori_loop(0, nk // k_unroll, update_body, 0)
        return carry

    lax.fori_loop(0, tb, step, 0)

    def renorm_body(kc, c):
        for kq in range(k_unroll):
            k = kc * k_unroll + kq
            wb = bcast(wend_ref, 0, k)
            for vb in range(nvb):
                idx = srow(k, vb)
                s_ref[idx, :] = s_ref[idx, :] * wb
        return c

    lax.fori_loop(0, nk // k_unroll, renorm_body, 0)

    @pl.when(j == pl.num_programs(1) - 1)
    def _():
        sT_ref[...] = s_ref[...]


def _rwkv_scan(v, a1, a2, b1, b2, wend, s0):
    t, nk, lanes = v.shape
    tb = RWKV_BLOCK
    nblk = t // tb
    blk_idx = lambda d, j: j + d * (nblk - 1 - 2 * j)
    shared = pl.BlockSpec((tb, nk, lanes), lambda d, j: (blk_idx(d, j), 0, 0))
    perdir = pl.BlockSpec((None, tb, nk, lanes), lambda d, j: (d, blk_idx(d, j), 0, 0))
    ends = pl.BlockSpec((None, 1, nk, lanes), lambda d, j: (d, blk_idx(d, j), 0, 0))
    state = pl.BlockSpec((None, nk * nk, lanes), lambda d, j: (d, 0, 0))
    return pl.pallas_call(
        functools.partial(_rwkv_scan_kernel, k_unroll=16),
        grid=(2, nblk),
        in_specs=[shared, perdir, perdir, perdir, perdir, ends, state],
        out_specs=[perdir, state],
        out_shape=[jax.ShapeDtypeStruct((2, t, nk, lanes), F32), jax.ShapeDtypeStruct((2, nk * nk, lanes), F32)],
        scratch_shapes=[pltpu.VMEM((nk * nk, lanes), F32)],
        compiler_params=_cparams("arbitrary", "arbitrary"),
        name="rwkv_scan",
    )(v, a1, a2, b1, b2, wend, s0)


def _rwkv_out_kernel(yf_ref, yb_ref, bonus_ref, g_ref, gng_ref, gnb_ref, em_ref, o_ref):
    y = yf_ref[0, 0] + yb_ref[0, 0]
    em = em_ref[...]
    dlt = y - _sel_right(y, em)
    var = _sel_right(dlt * dlt, em)
    yn = dlt * lax.rsqrt(var + RWKV_GN_EPS) * gng_ref[...] + gnb_ref[...]
    o_ref[0] = (yn + bonus_ref[0]) * g_ref[0]


def _rwkv_out(y2, bonus, g, gn_g, gn_b, em, *, tt):
    _, bsz, t, wdt = y2.shape
    c2 = lambda b, j: (0, 0)
    tok = lambda b, j: (b, j, 0)
    return pl.pallas_call(
        _rwkv_out_kernel,
        grid=(bsz, t // tt),
        in_specs=[pl.BlockSpec((1, 1, tt, wdt), lambda b, j: (0, b, j, 0)),
                  pl.BlockSpec((1, 1, tt, wdt), lambda b, j: (1, b, j, 0)),
                  pl.BlockSpec((1, tt, wdt), tok), pl.BlockSpec((1, tt, wdt), tok),
                  pl.BlockSpec((1, wdt), c2), pl.BlockSpec((1, wdt), c2), pl.BlockSpec((wdt, wdt), c2)],
        out_specs=pl.BlockSpec((1, tt, wdt), tok),
        out_shape=jax.ShapeDtypeStruct((bsz, t, wdt), F32),
        compiler_params=_cparams("parallel", "parallel"),
        name="rwkv_out",
    )(y2, y2, bonus, g, gn_g, gn_b, em)


def _to_time_major(a):
    *lead, bsz, t, _ = a.shape
    nl = len(lead)
    a = a.reshape(*lead, bsz, t, RWKV_HEADS, RWKV_HEAD_DIM)
    a = a.transpose(*range(nl), nl + 1, nl + 3, nl, nl + 2)
    return a.reshape(*lead, t, RWKV_HEAD_DIM, bsz * RWKV_HEADS)


def _from_time_major(a, bsz):
    two, t, n, _ = a.shape
    a = a.reshape(two, t, n, bsz, RWKV_HEADS).transpose(0, 3, 1, 4, 2)
    return a.reshape(two, bsz, t, RWKV_HEADS * n)


def _na_kernel(q_ref, k_ref, v_ref, qc_ref, kc_ref, vc_ref, bias_ref, y_ref, yc_ref, kb_s, vb_s, kcb_s, vcb_s, *, rows):
    scale = NA_HEAD_DIM ** -0.5
    wq = GRID_W
    band = NA_WIN_R * GRID_W
    kb_s[...] = k_ref[0].astype(BF16)
    vb_s[...] = v_ref[0].astype(BF16)
    kcb_s[...] = kc_ref[0].astype(BF16)
    vcb_s[...] = vc_ref[0].astype(BF16)
    low = _iota((1, LANES), 1) < NA_HEAD_DIM

    def stack_heads(qs):
        return jnp.concatenate([jnp.where(low, qs, 0.0), jnp.where(low, 0.0, qs)], axis=0).astype(BF16)

    def softmax_parts(scores):
        m = functools.reduce(jnp.maximum, [jnp.max(s, axis=-1, keepdims=True) for s in scores])
        ps = [jnp.exp(s - m) for s in scores]
        l = functools.reduce(lambda a, b: a + b, [jnp.sum(p, axis=-1, keepdims=True) for p in ps])
        return ps, l

    def combine(ps, l, values):
        o = functools.reduce(lambda a, b: a + b, [_mm(p, v) for p, v in zip(ps, values)]) / l
        m = o.shape[0] // 2
        return jnp.where(low, o[0:m], o[m:2 * m])

    def row_group(i, c):
        jobs = []
        for u in range(NA_ROW_GROUP):
            r = i * NA_ROW_GROUP + u
            rs = jnp.clip(r - NA_WIN_R // 2, 0, rows - NA_WIN_R)
            var = jnp.where(r < NA_WIN_R // 2, r,
                            jnp.where(r > rows - NA_WIN_R // 2, r - (rows - NA_WIN_R), NA_WIN_R // 2))
            q0 = pl.multiple_of(r * wq, wq)
            k0 = pl.multiple_of(rs * wq, wq)
            qst = stack_heads(q_ref[0, pl.ds(q0, wq), :] * scale)
            scores = [_mm_nt(qst, kb_s[pl.ds(k0, band), :]) + bias_ref[0, var], _mm_nt(qst, kcb_s[...])]
            jobs.append((q0, k0, scores))
        soft = [softmax_parts(scores) for _, _, scores in jobs]
        for (q0, k0, _), (ps, l) in zip(jobs, soft):
            y_ref[0, pl.ds(q0, wq), :] = combine(ps, l, [vb_s[pl.ds(k0, band), :], vcb_s[...]])
        return c

    lax.fori_loop(0, rows // NA_ROW_GROUP, row_group, 0)
    ps, l = softmax_parts([_mm_nt(stack_heads(qc_ref[0] * scale), kcb_s[...])])
    yc_ref[0] = combine(ps, l, [vcb_s[...]])


def _na_bias_table(rpb, rows):
    cols = jnp.arange(GRID_W)
    col_start = jnp.clip(cols - NA_WIN_C // 2, 0, GRID_W - NA_WIN_C)
    col_in = (cols[None, :] >= col_start[:, None]) & (cols[None, :] < col_start[:, None] + NA_WIN_C)
    dc_idx = jnp.clip(cols[None, :] - cols[:, None] + NA_WIN_C - 1, 0, 2 * NA_WIN_C - 2)
    rpb_cols = jnp.where(col_in[None, None], rpb[:, :, dc_idx], -jnp.inf)
    half = NA_WIN_R // 2
    rep_rows = list(range(half)) + [half] + list(range(rows - half + 1, rows))
    tiles = []
    for r in rep_rows:
        rs = min(max(r - half, 0), rows - NA_WIN_R)
        dr_idx = rs + jnp.arange(NA_WIN_R) - r + NA_WIN_R - 1
        t = rpb_cols[:, dr_idx]
        tiles.append(t.transpose(0, 2, 1, 3).reshape(NA_HEADS, GRID_W, NA_WIN_R * GRID_W))
    tab = jnp.stack(tiles, axis=1)
    tab = tab.reshape(NA_HEADS // 2, 2, len(rep_rows), GRID_W, NA_WIN_R * GRID_W).transpose(0, 2, 1, 3, 4)
    return tab.reshape(NA_HEADS // 2, len(rep_rows), 2 * GRID_W, NA_WIN_R * GRID_W)


def _na(u, uc, bias_tab, col0):
    bsz, s, _ = u.shape
    lc = uc.shape[1]
    rows = s // GRID_W
    assert rows >= 2 * NA_WIN_R and rows % NA_ROW_GROUP == 0
    qb, kb, vb = col0 // LANES, col0 // LANES + 4, col0 // LANES + 8
    band = NA_WIN_R * GRID_W
    lat = lambda cb: pl.BlockSpec((1, s, LANES), lambda b, p: (b, 0, cb + p))
    cx = lambda cb: pl.BlockSpec((1, lc, LANES), lambda b, p: (b, 0, cb + p))
    return pl.pallas_call(
        functools.partial(_na_kernel, rows=rows),
        grid=(bsz, NA_HEADS // 2),
        in_specs=[lat(qb), lat(kb), lat(vb), cx(qb), cx(kb), cx(vb),
                  pl.BlockSpec((1, NA_WIN_R, 2 * GRID_W, band), lambda b, p: (p, 0, 0, 0))],
        out_specs=[pl.BlockSpec((1, s, LANES), lambda b, p: (b, 0, p)),
                   pl.BlockSpec((1, lc, LANES), lambda b, p: (b, 0, p))],
        out_shape=[jax.ShapeDtypeStruct((bsz, s, NA_HEADS * NA_HEAD_DIM), F32),
                   jax.ShapeDtypeStruct((bsz, lc, NA_HEADS * NA_HEAD_DIM), F32)],
        scratch_shapes=[pltpu.VMEM((s, LANES), BF16), pltpu.VMEM((s, LANES), BF16),
                        pltpu.VMEM((lc, LANES), BF16), pltpu.VMEM((lc, LANES), BF16)],
        compiler_params=_cparams("parallel", "arbitrary"),
        name="na",
    )(u, u, u, uc, uc, uc, bias_tab)


def _gla_kernel(q_ref, k_ref, v_ref, r_ref, gd_ref, kc_ref, vc_ref, gdc_ref, g2_ref, gb_ref, ng_ref, y_ref,
                bc_s, qt_s, kt_s, dec_s, ktc_s, decc_s, st_s, of_s):
    s_len = q_ref.shape[1]
    l_len = kc_ref.shape[1]
    cs = GLA_SUB
    scale = GLA_DK ** -0.5
    ri = _iota((LANES, LANES), 0)
    ci = _iota((LANES, LANES), 1)
    sub_shift = cs.bit_length() - 1
    same_chunk = (ri >> sub_shift) == (ci >> sub_shift)
    lane = _iota((1, LANES), 1)
    head_lane = [lane < GLA_DK, lane >= GLA_DK]
    st_mask = (_iota((2 * GLA_DV, LANES), 0) >= GLA_DV) == (_iota((2 * GLA_DV, LANES), 1) >= GLA_DK)
    trow = _iota((cs, 1), 0)

    cum_sels = [jnp.concatenate([(same_chunk & ((ci <= ri) if d == 0 else (ci >= ri))).astype(F32),
                                 same_chunk.astype(F32)], axis=0).astype(BF16) for d in range(2)]

    def gates(gd):
        gs = [_log_sigmoid(_mm(gd, g2_ref[d, 0]) + gb_ref[d, 0]) / GLA_GATE_NORM for d in range(2)]
        ys = [_sel_left(cum_sels[d], gs[d]) for d in range(2)]
        return [(y[0:LANES], y[LANES:2 * LANES]) for y in ys]

    def prep_lat(i, c):
        rows = pl.ds(pl.multiple_of(i * LANES, LANES), LANES)
        qs = q_ref[0, rows, :] * scale
        kk = k_ref[0, rows, :]
        for d, (bc, tot) in enumerate(gates(gd_ref[0, rows, :])):
            bc_s[d, rows, :] = bc
            qt_s[d, rows, :] = qs * jnp.exp(bc)
            kt_s[d, rows, :] = kk * jnp.exp(tot - bc)
            dec_s[d, rows, :] = jnp.exp(tot)
        return c

    def prep_ctx(i, c):
        rows = pl.ds(pl.multiple_of(i * LANES, LANES), LANES)
        kk = kc_ref[0, rows, :]
        for d, (bc, tot) in enumerate(gates(gdc_ref[0, rows, :])):
            ktc_s[d, rows, :] = kk * jnp.exp(tot - bc)
            decc_s[d, rows, :] = jnp.exp(tot)
        return c

    lax.fori_loop(0, s_len // LANES, prep_lat, 0, unroll=2)
    lax.fori_loop(0, l_len // LANES, prep_ctx, 0, unroll=2)

    for d in range(2):
        fwd = d == 0
        st_s[...] = jnp.zeros(st_s.shape, F32)

        grp = GLA_GROUP
        order = tuple(range(grp)) if fwd else tuple(range(grp - 1, -1, -1))

        def group_starts(i, n_groups, fwd=fwd):
            g = i if fwd else n_groups - 1 - i
            return [(g * grp + u) * cs for u in range(grp)]

        def group_rows(i, n_groups):
            return [pl.ds(pl.multiple_of(r0, cs), cs) for r0 in group_starts(i, n_groups)]

        def increment(kt, vv):
            return jnp.where(st_mask, _mm_tn(vv, kt), 0.0)

        def ctx_group(i, c, order=order):
            rows = group_rows(i, l_len // (cs * grp))
            incs = [increment(ktc_s[d, rows[u], :], vc_ref[0, rows[u], :]) for u in range(grp)]
            st = st_s[...]
            for u in order:
                st = st * decc_s[d, rows[u], :][0:1, :] + incs[u]
            st_s[...] = st
            return c

        def lat_group(i, c, fwd=fwd, order=order):
            starts = group_starts(i, s_len // (cs * grp))
            rows = [pl.ds(pl.multiple_of(r0, cs), cs) for r0 in starts]
            vvs =[v_ref[0, rows[u], :] for u in range(grp)]
            incs = [increment(kt_s[d, rows[u], :], vvs[u]) for u in range(grp)]
            st = st_s[...]
            inter = [None] * grp
            for u in order:
                inter[u] = _mm_nt(qt_s[d, rows[u], :], st)
                st = st * dec_s[d, rows[u], :][0:1, :] + incs[u]
            st_s[...] = st
            rowb = lambda x: jnp.broadcast_to(x, (cs, x.shape[-1]))
            for u in range(grp):
                bc = bc_s[d, rows[u], :]
                qs = q_ref[0, rows[u], :] * scale
                o0, o1 = inter[u][:, 0:GLA_DV], inter[u][:, GLA_DV:2 * GLA_DV]
                for s in range(cs):
                    srow = pl.ds(starts[u] + s, 1)
                    keep = (trow >= s) if fwd else (trow <= s)
                    e = jnp.exp(jnp.where(keep, bc - rowb(bc_s[d, srow, :]), -jnp.inf))
                    term = qs * rowb(k_ref[0, srow, :]) * e
                    a0 = jnp.sum(jnp.where(head_lane[0], term, 0.0), axis=-1, keepdims=True)
                    a1 = jnp.sum(jnp.where(head_lane[1], term, 0.0), axis=-1, keepdims=True)
                    vrow = rowb(v_ref[0, srow, :])
                    o0 = o0 + a0 * vrow[:, 0:GLA_DV]
                    o1 = o1 + a1 * vrow[:, GLA_DV:2 * GLA_DV]
                if fwd:
                    of_s[rows[u], 0:GLA_DV] = o0
                    of_s[rows[u], GLA_DV:2 * GLA_DV] = o1
                else:
                    rr = r_ref[0, rows[u], :]
                    for h, oh in enumerate((o0, o1)):
                        cols = slice(h * GLA_DV, (h + 1) * GLA_DV)
                        ot = of_s[rows[u], cols] + oh
                        on = ot * lax.rsqrt(jnp.mean(ot * ot, axis=-1, keepdims=True) + RMS_EPS) * ng_ref[...]
                        rh = rr[:, cols]
                        y_ref[0, rows[u], cols] = on * (rh * _sigmoid(rh))
            return c

        lax.fori_loop(0, l_len // (cs * grp), ctx_group, 0)
        lax.fori_loop(0, s_len // (cs * grp), lat_group, 0)


def _gla(u, uc, g2p, gbp, norm_g, gd_block):
    bsz, s, _ = u.shape
    lc = uc.shape[1]
    pairs = GLA_HEADS // 2
    dv2 = 2 * GLA_DV
    lat = lambda width, blk: pl.BlockSpec((1, s, width), lambda b, p: (b, 0, blk(p)))
    cx = lambda width, blk: pl.BlockSpec((1, lc, width), lambda b, p: (b, 0, blk(p)))
    q_blk = lambda p: p
    k_blk = lambda p: pairs + p
    v_blk = lambda p: 2 * pairs * LANES // dv2 + p
    r_blk = lambda p: (2 * pairs * LANES + GLA_HEADS * GLA_DV) // dv2 + p
    gd_blk = lambda p: gd_block
    return pl.pallas_call(
        _gla_kernel,
        grid=(bsz, pairs),
        in_specs=[lat(LANES, q_blk), lat(LANES, k_blk), lat(dv2, v_blk), lat(dv2, r_blk), lat(LANES, gd_blk),
                  cx(LANES, k_blk), cx(dv2, v_blk), cx(LANES, gd_blk),
                  pl.BlockSpec((2, 1, LANES, LANES), lambda b, p: (0, p, 0, 0)),
                  pl.BlockSpec((2, 1, 1, LANES), lambda b, p: (0, p, 0, 0)),
                  pl.BlockSpec((1, GLA_DV), lambda b, p: (0, 0))],
        out_specs=pl.BlockSpec((1, s, dv2), lambda b, p: (b, 0, p)),
        out_shape=jax.ShapeDtypeStruct((bsz, s, GLA_HEADS * GLA_DV), F32),
        scratch_shapes=[pltpu.VMEM((2, s, LANES), F32)] * 4 + [pltpu.VMEM((2, lc, LANES), F32)] * 2
                       + [pltpu.VMEM((dv2, LANES), F32), pltpu.VMEM((s, dv2), F32)],
        compiler_params=_cparams("parallel", "arbitrary"),
        name="gla",
    )(u, u, u, u, u, uc, uc, uc, g2p, gbp, norm_g)


def _diff_kernel(q_ref, k_ref, v_ref, kc_ref, vc_ref, lam_ref, ng_ref, y_ref, k_s, v_s, *, lambda_init):
    scale = DIFF_DH ** -0.5
    s_len = k_ref.shape[1]

    @pl.when(pl.program_id(2) == 0)
    def _():
        k_s[0:s_len, :] = k_ref[0].astype(BF16)
        v_s[0:s_len, :] = v_ref[0].astype(BF16)
        k_s[s_len:, :] = kc_ref[0].astype(BF16)
        v_s[s_len:, :] = vc_ref[0].astype(BF16)

    lp = lam_ref[...]
    lam = (jnp.exp(jnp.sum(lp[0:1] * lp[1:2], axis=-1, keepdims=True))
           - jnp.exp(jnp.sum(lp[2:3] * lp[3:4], axis=-1, keepdims=True)) + lambda_init)
    qs = q_ref[0] * scale
    low = _iota((1, LANES), 1) < DIFF_DH
    scores = [_mm_nt(jnp.where(low, qs, 0.0) if m == 0 else jnp.where(low, 0.0, qs), k_s[...]) for m in range(2)]
    probs = [jnp.exp(s - jnp.max(s, axis=-1, keepdims=True)) for s in scores]
    outs = [_mm(p, v_s[...]) / jnp.sum(p, axis=-1, keepdims=True) for p in probs]
    o = outs[0] - lam * outs[1]
    on = o * lax.rsqrt(jnp.mean(o * o, axis=-1, keepdims=True) + RMS_EPS) * ng_ref[...]
    y_ref[0] = on * (1.0 - lambda_init)


def _diff(u, uc, lam_params, norm_g, col0, lambda_init, *, tq):
    bsz, s, _ = u.shape
    lc = uc.shape[1]
    qb = col0 // LANES
    kb, vb = qb + DIFF_HEADS, qb + 2 * DIFF_HEADS
    full = lambda arr_len, cb: pl.BlockSpec((1, arr_len, LANES), lambda b, h, j: (b, 0, cb + h))
    return pl.pallas_call(
        functools.partial(_diff_kernel, lambda_init=lambda_init),
        grid=(bsz, DIFF_HEADS, s // tq),
        in_specs=[pl.BlockSpec((1, tq, LANES), lambda b, h, j: (b, j, qb + h)),
                  full(s, kb), full(s, vb), full(lc, kb), full(lc, vb),
                  pl.BlockSpec((4, DIFF_DH), lambda b, h, j: (0, 0)),
                  pl.BlockSpec((1, DIFF_DV), lambda b, h, j: (0, 0))],
        out_specs=pl.BlockSpec((1, tq, LANES), lambda b, h, j: (b, j, h)),
        out_shape=jax.ShapeDtypeStruct((bsz, s, DIFF_HEADS * DIFF_DV), F32),
        scratch_shapes=[pltpu.VMEM((s + lc, LANES), BF16), pltpu.VMEM((s + lc, LANES), BF16)],
        compiler_params=_cparams("parallel", "parallel", "arbitrary"),
        name="diff_attn",
    )(u, u, u, uc, uc, lam_params, norm_g)


def _head_selector(width, head_dim, value):
    idx = jnp.arange(width) // head_dim
    return jnp.where(idx[:, None] == idx[None, :], value, 0.0).astype(F32)


def _pad_rows(w, lo, total):
    return jnp.zeros((total, w.shape[1]), w.dtype).at[lo:lo + w.shape[0]].set(w)


def _rope_tables(s, reps):
    pos = jnp.arange(s)
    n = DIFF_DH // 4
    freqs = ROPE_BASE ** (-jnp.arange(n, dtype=F32) / n)
    ang_r = (pos // GRID_W).astype(F32)[:, None] * freqs[None, :]
    ang_c = (pos % GRID_W).astype(F32)[:, None] * freqs[None, :]
    cos = jnp.concatenate([jnp.cos(ang_r)] * 2 + [jnp.cos(ang_c)] * 2, axis=-1)
    sin = jnp.concatenate([-jnp.sin(ang_r), jnp.sin(ang_r), -jnp.sin(ang_c), jnp.sin(ang_c)], axis=-1)
    return jnp.tile(cos, (1, reps)), jnp.tile(sin, (1, reps))


def _rwkv_na_mixer(xl, xc, mods, ctx_row, w_in, mu, w0, w2, a0, a2, g2, k_k, k_a, r_k, gn_g, gn_b, rpb):
    bsz, s, _ = xl.shape
    lc = xc.shape[1]
    n = w_in.shape[1]
    chunks = tuple((lo, min(lo + 512, n)) for lo in range(0, n, 512))
    u = _proj(xl, mods, w_in, chunks, tm=512)
    uc = _proj(xc, mods, w_in, chunks, tm=lc, ctx_row=ctx_row)

    eh = _head_selector(RWKV_WIDTH, RWKV_HEAD_DIM, 1.0).astype(BF16)
    em = _head_selector(RWKV_WIDTH, RWKV_HEAD_DIM, 1.0 / RWKV_HEAD_DIM).astype(BF16)
    p = dict(mu=mu[None], w0=w0, a0=a0, g2=g2.astype(BF16), k_k=k_k[None], k_a=k_a[None], r_k=r_k.reshape(1, -1), eh=eh,
             w2=jnp.stack([_pad_rows(w2[0], 0, 128), _pad_rows(w2[1], 64, 128)]).astype(BF16),
             a2=jnp.stack([_pad_rows(a2[0], 0, 128), _pad_rows(a2[1], 64, 128)]).astype(BF16))
    lanes = bsz * RWKV_HEADS
    state = jnp.zeros((2, RWKV_HEAD_DIM * RWKV_HEAD_DIM, lanes), F32)
    ys = []
    for uu, tt in ((uc, lc), (u, 256)):
        v, a1, a2, b1, b2, wend, bonus, g = _rwkv_prep(uu, p, tt=tt)
        y2, state = _rwkv_scan(_to_time_major(v), _to_time_major(a1), _to_time_major(a2), _to_time_major(b1),
                               _to_time_major(b2), _to_time_major(wend), state)
        ys.append(_rwkv_out(_from_time_major(y2, bsz), bonus, g, gn_g[None], gn_b[None], em, tt=tt))
    yc_a, y_a = ys
    y_b, yc_b = _na(u, uc, _na_bias_table(rpb, s // GRID_W), RWKV_IN)
    return (y_a, y_b), (yc_a, yc_b)


def _gla_diff_mixer(xl, xc, mods, ctx_row, w_in, gla_g2, gla_gb, gla_norm_g, diff_lambda, diff_norm_g, lambda_init):
    bsz, s, d = xl.shape
    lc = xc.shape[1]
    n_gla = 2 * GLA_HEADS * GLA_DK + 2 * GLA_HEADS * GLA_DV
    n_gate = 2 * GLA_GATE_LORA
    n_diff = 2 * DIFF_HEADS * 2 * DIFF_DH + DIFF_HEADS * DIFF_DV
    w_perm = jnp.concatenate([w_in[:, :n_gla], w_in[:, n_gla + n_gate:n_gla + n_gate + n_diff],
                              w_in[:, n_gla:n_gla + n_gate], jnp.zeros((d, LANES - n_gate), w_in.dtype)], axis=1).astype(BF16)
    rope_w = 2 * DIFF_HEADS * 2 * DIFF_DH
    chunks = ((0, 512), (512, 1024), (1024, n_gla), (n_gla, n_gla + rope_w),
              (n_gla + rope_w, n_gla + n_diff), (n_gla + n_diff, n_gla + n_diff + LANES))
    cos_t, sin_t = _rope_tables(s, rope_w // DIFF_DH)
    u = _proj(xl, mods, w_perm, chunks, tm=512, rope=(3, cos_t, sin_t))
    uc = _proj(xc, mods, w_perm, chunks, tm=lc, ctx_row=ctx_row)

    pairs = GLA_HEADS // 2
    g2p = jnp.stack([jnp.stack([_pad_rows(gla_g2[dd][:, p * LANES:(p + 1) * LANES], dd * GLA_GATE_LORA, LANES)
                                for p in range(pairs)]) for dd in range(2)]).astype(BF16)
    gbp = gla_gb.reshape(2, pairs, 1, LANES)
    y_c = _gla(u, uc, g2p, gbp, gla_norm_g[None], (n_gla + n_diff) // LANES)
    y_d = _diff(u, uc, diff_lambda, diff_norm_g[None], n_gla, lambda_init, tq=512)
    return y_c, y_d


def kernel(x, c, ctx, c_ctx, w_mod_0, b_mod_0, ln_g_0, ln_b_0, ffn_gu_0, ffn_down_0, w_in_0, w_out_0, rwkv_mu_0, rwkv_w0_0, rwkv_w2_0, rwkv_a0_0, rwkv_a2_0, rwkv_g2_0, rwkv_k_k_0, rwkv_k_a_0, rwkv_r_k_0, rwkv_gn_g_0, rwkv_gn_b_0, na_rpb_0, w_mod_1, b_mod_1, ln_g_1, ln_b_1, ffn_gu_1, ffn_down_1, w_in_1, w_out_1, gla_g2_1, gla_gb_1, gla_norm_g_1, diff_lambda_1, diff_norm_g_1):
    bsz, s, d = x.shape
    lc = ctx.shape[1]
    ctx_row = bsz
    rows = -(-(bsz + 1) // SUBLANES) * SUBLANES
    c_all = jnp.concatenate([c, c_ctx[None], jnp.zeros((rows - bsz - 1, d), F32)], axis=0)
    tm = 512
    tf = 512

    mods = _mods(c_all, w_mod_0, b_mod_0)
    gu, dn = ffn_gu_0.astype(BF16), ffn_down_0.astype(BF16)
    xl = _ffn(x, mods, 0, gu[0], dn[0], ln_g_0[0], ln_b_0[0], tm=tf)
    xc = _ffn(ctx, mods, 0, gu[0], dn[0], ln_g_0[0], ln_b_0[0], tm=lc, ctx_row=ctx_row)
    (y_a, y_b), (yc_a, yc_b) = _rwkv_na_mixer(xl, xc, mods, ctx_row, w_in_0.astype(BF16), rwkv_mu_0, rwkv_w0_0, rwkv_w2_0,
                                              rwkv_a0_0, rwkv_a2_0, rwkv_g2_0, rwkv_k_k_0, rwkv_k_a_0, rwkv_r_k_0,
                                              rwkv_gn_g_0, rwkv_gn_b_0, na_rpb_0)
    wo = w_out_0.astype(BF16)
    xl = _mixout(xl, mods, y_a, y_b, wo, ln_g_0[1], ln_b_0[1], tm=tm)
    xc = _mixout(xc, mods, yc_a, yc_b, wo, ln_g_0[1], ln_b_0[1], tm=lc, ctx_row=ctx_row)
    xl = _ffn(xl, mods, 6, gu[1], dn[1], ln_g_0[2], ln_b_0[2], tm=tf)
    xc = _ffn(xc, mods, 6, gu[1], dn[1], ln_g_0[2], ln_b_0[2], tm=lc, ctx_row=ctx_row)

    mods = _mods(c_all, w_mod_1, b_mod_1)
    gu, dn = ffn_gu_1.astype(BF16), ffn_down_1.astype(BF16)
    xl = _ffn(xl, mods, 0, gu[0], dn[0], ln_g_1[0], ln_b_1[0], tm=tf)
    xc = _ffn(xc, mods, 0, gu[0], dn[0], ln_g_1[0], ln_b_1[0], tm=lc, ctx_row=ctx_row)
    y_c, y_d = _gla_diff_mixer(xl, xc, mods, ctx_row, w_in_1, gla_g2_1, gla_gb_1, gla_norm_g_1, diff_lambda_1, diff_norm_g_1,
                               0.8 - 0.6 * math.exp(-0.3 * 1))
    xl = _mixout(xl, mods, y_c, y_d, w_out_1.astype(BF16), ln_g_1[1], ln_b_1[1], tm=tm)
    xl = _ffn(xl, mods, 6, gu[1], dn[1], ln_g_1[2], ln_b_1[2], tm=tf)
    return xl
```

```python
import functools
import math

import jax
import jax.numpy as jnp
from jax import lax
from jax.experimental import pallas as pl
from jax.experimental.pallas import tpu as pltpu

F32 = jnp.float32
BF16 = jnp.bfloat16

GRID_W = 64
N_MOD = 9
DEPTH = 2
LN_EPS = 1e-5
RMS_EPS = 1e-6
DEEPNORM_ALPHA = (2 * DEPTH) ** 0.25
RWKV_HEADS = 8
RWKV_HEAD_DIM = 64
RWKV_WIDTH = RWKV_HEADS * RWKV_HEAD_DIM
RWKV_IN = 3 * RWKV_WIDTH + 4 * 64 + 128
RWKV_GN_EPS = 64e-5
RWKV_BLOCK = 16
NA_HEADS = 8
NA_HEAD_DIM = 64
NA_WIN_R = 8
NA_WIN_C = 16
NA_ROW_GROUP = 4
GLA_HEADS = 4
GLA_DK = 64
GLA_DV = 128
GLA_GATE_LORA = 16
GLA_GATE_NORM = 16.0
GLA_SUB = 16
GLA_GROUP = 8
DIFF_HEADS = 4
DIFF_DH = 64
DIFF_DV = 128
DIFF_Q_PARTS = 2
ROPE_BASE = 10000.0

LANES = 128
SUBLANES = 8
V7X_VMEM_BYTES = 64 * 1024 * 1024
VMEM_LIMIT = (V7X_VMEM_BYTES * 7) // 8


def _cparams(*sem):
    return pltpu.CompilerParams(dimension_semantics=sem, vmem_limit_bytes=VMEM_LIMIT)


def _mm(a, b):
    return jnp.dot(a.astype(BF16), b.astype(BF16), preferred_element_type=F32)


def _mm_nt(a, b):
    return lax.dot_general(a.astype(BF16), b.astype(BF16), (((1,), (1,)), ((), ())), preferred_element_type=F32)


def _mm_tn(a, b):
    return lax.dot_general(a.astype(BF16), b.astype(BF16), (((0,), (0,)), ((), ())), preferred_element_type=F32)


def _split3(x):
    hi = x.astype(BF16)
    r1 = x - hi.astype(F32)
    mid = r1.astype(BF16)
    lo = (r1 - mid.astype(F32)).astype(BF16)
    return hi, mid, lo


def _sel_right(x, sel):
    m = x.shape[0]
    y = jnp.dot(jnp.concatenate(_split3(x), axis=0), sel, preferred_element_type=F32)
    return y[0:m] + y[m:2 * m] + y[2 * m:3 * m]


def _sel_left(sel, x):
    n = x.shape[1]
    y = jnp.dot(sel, jnp.concatenate(_split3(x), axis=1), preferred_element_type=F32)
    return y[:, 0:n] + y[:, n:2 * n] + y[:, 2 * n:3 * n]


def _sigmoid(x):
    return 1.0 / (1.0 + jnp.exp(-x))


def _log_sigmoid(x):
    return -(jnp.maximum(-x, 0.0) + jnp.log(1.0 + jnp.exp(-jnp.abs(x))))


def _layer_norm(z, g, b):
    mu = jnp.mean(z, axis=-1, keepdims=True)
    d = z - mu
    var = jnp.mean(d * d, axis=-1, keepdims=True)
    return d * lax.rsqrt(var + LN_EPS) * g + b


def _iota(shape, dim):
    return lax.broadcasted_iota(jnp.int32, shape, dim)


def _mods_kernel(c_ref, w_ref, b_ref, o_ref):
    c = c_ref[...]
    o_ref[...] = _mm(c * _sigmoid(c), w_ref[...]) + b_ref[...]


def _mods(c_all, w_mod, b_mod):
    rows, d = c_all.shape
    n = w_mod.shape[1]
    tn = n // 8
    out = pl.pallas_call(
        _mods_kernel,
        grid=(n // tn,),
        in_specs=[pl.BlockSpec((rows, d), lambda j: (0, 0)),
                  pl.BlockSpec((d, tn), lambda j: (0, j)),
                  pl.BlockSpec((1, tn), lambda j: (0, j))],
        out_specs=pl.BlockSpec((rows, tn), lambda j: (0, j)),
        out_shape=jax.ShapeDtypeStruct((rows, n), F32),
        compiler_params=_cparams("arbitrary"),
        name="mods",
    )(c_all, w_mod, b_mod[None])
    return out.reshape(rows, N_MOD, d)


def _mod_spec(d, ctx_row):
    if ctx_row is None:
        return pl.BlockSpec((1, N_MOD, d), lambda b, j: (b, 0, 0))
    return pl.BlockSpec((1, N_MOD, d), lambda b, j: (ctx_row, 0, 0))


def _ffn_kernel(x_ref, m_ref, wgu_ref, wd_ref, g_ref, b_ref, o_ref, *, i0, fc):
    hidden = wd_ref.shape[0]
    x = x_ref[0]
    shift, scale, gate = (m_ref[0, i0 + i:i0 + i + 1, :] for i in range(3))
    h = (x * (1.0 + scale) + shift).astype(BF16)
    acc = None
    for c in range(hidden // fc):
        g = jnp.dot(h, wgu_ref[:, c * fc:(c + 1) * fc], preferred_element_type=F32)
        u = jnp.dot(h, wgu_ref[:, hidden + c * fc:hidden + (c + 1) * fc], preferred_element_type=F32)
        a = (g * _sigmoid(g) * u).astype(BF16)
        dn = jnp.dot(a, wd_ref[c * fc:(c + 1) * fc, :], preferred_element_type=F32)
        acc = dn if acc is None else acc + dn
    z = DEEPNORM_ALPHA * x + gate * (0.5 * acc)
    o_ref[0] = _layer_norm(z, g_ref[...], b_ref[...])


def _ffn(x, mods, i0, w_gu, w_down, ln_g, ln_b, *, tm, ctx_row=None):
    bsz, t, d = x.shape
    hidden = w_down.shape[0]
    const = lambda b, j: (0, 0)
    return pl.pallas_call(
        functools.partial(_ffn_kernel, i0=i0, fc=256),
        grid=(bsz, t // tm),
        in_specs=[pl.BlockSpec((1, tm, d), lambda b, j: (b, j, 0)),
                  _mod_spec(d, ctx_row),
                  pl.BlockSpec((d, 2 * hidden), const, pipeline_mode=pl.Buffered(1)),
                  pl.BlockSpec((hidden, d), const, pipeline_mode=pl.Buffered(1)),
                  pl.BlockSpec((1, d), const),
                  pl.BlockSpec((1, d), const)],
        out_specs=pl.BlockSpec((1, tm, d), lambda b, j: (b, j, 0)),
        out_shape=jax.ShapeDtypeStruct(x.shape, F32),
        compiler_params=_cparams("parallel", "parallel"),
        name="ffn",
    )(x, mods, w_gu, w_down, ln_g[None], ln_b[None])


def _proj_kernel(x_ref, m_ref, w_ref, *rest, chunks, rope_chunk):
    if rope_chunk is None:
        (o_ref,) = rest
    else:
        cos_ref, sin_ref, o_ref = rest
    x = x_ref[0]
    h = (x * (1.0 + m_ref[0, 4:5, :]) + m_ref[0, 3:4, :]).astype(BF16)
    for ci, (lo, hi) in enumerate(chunks):
        u = jnp.dot(h, w_ref[:, lo:hi], preferred_element_type=F32)
        if ci == rope_chunk:
            n = hi - lo
            first = (_iota((1, n), 1) & 16) == 0
            partner = jnp.where(first, pltpu.roll(u, n - 16, axis=1), pltpu.roll(u, 16, axis=1))
            u = u * cos_ref[...] + partner * sin_ref[...]
        o_ref[0, :, lo:hi] = u


def _proj(x, mods, w_in, chunks, *, tm, ctx_row=None, rope=None):
    bsz, t, d = x.shape
    n = w_in.shape[1]
    const = lambda b, j: (0, 0)
    in_specs = [pl.BlockSpec((1, tm, d), lambda b, j: (b, j, 0)),
                _mod_spec(d, ctx_row),
                pl.BlockSpec((d, n), const, pipeline_mode=pl.Buffered(1))]
    args = [x, mods, w_in]
    rope_chunk = None
    if rope is not None:
        rope_chunk, cos_t, sin_t = rope
        width = cos_t.shape[1]
        in_specs += [pl.BlockSpec((tm, width), lambda b, j: (j, 0))] * 2
        args += [cos_t, sin_t]
    return pl.pallas_call(
        functools.partial(_proj_kernel, chunks=chunks, rope_chunk=rope_chunk),
        grid=(bsz, t // tm),
        in_specs=in_specs,
        out_specs=pl.BlockSpec((1, tm, n), lambda b, j: (b, j, 0)),
        out_shape=jax.ShapeDtypeStruct((bsz, t, n), F32),
        compiler_params=_cparams("parallel", "parallel"),
        name="proj",
    )(*args)


def _mixout_kernel(x_ref, m_ref, ya_ref, yb_ref, w_ref, g_ref, b_ref, o_ref):
    x = x_ref[0]
    half = ya_ref.shape[2]
    y = _mm(ya_ref[0], w_ref[0:half, :]) + _mm(yb_ref[0], w_ref[half:2 * half, :])
    z = DEEPNORM_ALPHA * x + m_ref[0, 5:6, :] * y
    o_ref[0] = _layer_norm(z, g_ref[...], b_ref[...])


def _rwkv_mixout_kernel(x_ref, m_ref, yf_ref, yr_ref, bonus_ref, gate_ref, gng_ref, gnb_ref, em_ref, yb_ref, w_ref,
                        g_ref, b_ref, o_ref):
    x = x_ref[0]
    half = yb_ref.shape[2]
    y = yf_ref[0, 0] + yr_ref[0, 0]
    em = em_ref[...]
    dlt = y - _sel_right(y, em)
    var = _sel_right(dlt * dlt, em)
    ya = (dlt * lax.rsqrt(var + RWKV_GN_EPS) * gng_ref[...] + gnb_ref[...] + bonus_ref[0]) * gate_ref[0]
    y = _mm(ya, w_ref[0:half, :]) + _mm(yb_ref[0], w_ref[half:2 * half, :])
    z = DEEPNORM_ALPHA * x + m_ref[0, 5:6, :] * y
    o_ref[0] = _layer_norm(z, g_ref[...], b_ref[...])


def _rwkv_mixout(x, mods, y2, bonus, gate, gn_g, gn_b, em, yb, w_out, ln_g, ln_b, *, tm, ctx_row=None):
    bsz, t, d = x.shape
    half = yb.shape[2]
    const = lambda b, j: (0, 0)
    tok = lambda b, j: (b, j, 0)
    head = pl.BlockSpec((1, tm, half), tok)
    return pl.pallas_call(
        _rwkv_mixout_kernel,
        grid=(bsz, t // tm),
        in_specs=[pl.BlockSpec((1, tm, d), tok), _mod_spec(d, ctx_row),
                  pl.BlockSpec((1, 1, tm, half), lambda b, j: (0, b, j, 0)),
                  pl.BlockSpec((1, 1, tm, half), lambda b, j: (1, b, j, 0)),
                  head, head, pl.BlockSpec((1, half), const), pl.BlockSpec((1, half), const),
                  pl.BlockSpec((half, half), const), head,
                  pl.BlockSpec((2 * half, d), const, pipeline_mode=pl.Buffered(1)),
                  pl.BlockSpec((1, d), const), pl.BlockSpec((1, d), const)],
        out_specs=pl.BlockSpec((1, tm, d), tok),
        out_shape=jax.ShapeDtypeStruct(x.shape, F32),
        compiler_params=_cparams("parallel", "parallel"),
        name="rwkv_mixout",
    )(x, mods, y2, y2, bonus, gate, gn_g, gn_b, em, yb, w_out, ln_g[None], ln_b[None])


def _mixout(x, mods, ya, yb, w_out, ln_g, ln_b, *, tm, ctx_row=None):
    bsz, t, d = x.shape
    half = ya.shape[2]
    const = lambda b, j: (0, 0)
    tok = lambda b, j: (b, j, 0)
    return pl.pallas_call(
        _mixout_kernel,
        grid=(bsz, t // tm),
        in_specs=[pl.BlockSpec((1, tm, d), tok), _mod_spec(d, ctx_row),
                  pl.BlockSpec((1, tm, half), tok), pl.BlockSpec((1, tm, half), tok),
                  pl.BlockSpec((2 * half, d), const, pipeline_mode=pl.Buffered(1)),
                  pl.BlockSpec((1, d), const), pl.BlockSpec((1, d), const)],
        out_specs=pl.BlockSpec((1, tm, d), tok),
        out_shape=jax.ShapeDtypeStruct(x.shape, F32),
        compiler_params=_cparams("parallel", "parallel"),
        name="mixout",
    )(x, mods, ya, yb, w_out, ln_g[None], ln_b[None])


def _rwkv_prep_kernel(u_ref, up_ref, un_ref, mu_ref, w0_ref, w2_ref, a0_ref, a2_ref, g2_ref, kk_ref, ka_ref, rk_ref,
                      eh_ref, cum_ref, v_o, a1_o, a2_o, b1_o, b2_o, wend_o, bonus_o, g_o, wfull_s, *, tt):
    j = pl.program_id(1)
    nj = pl.num_programs(1)
    wdt = RWKV_WIDTH
    u = u_ref[0]
    prev_row = up_ref[0, SUBLANES - 1:SUBLANES, :] * (j > 0).astype(F32)
    next_row = un_ref[0, 0:1, :] * (j < nj - 1).astype(F32)
    row = _iota((tt, 1), 0)
    up = jnp.where(row == 0, prev_row, pltpu.roll(u, 1, axis=0))
    un = jnp.where(row == tt - 1, next_row, pltpu.roll(u, tt - 1, axis=0))
    us = u + mu_ref[...] * (0.5 * (up + un) - u)
    r, k, v = us[:, 0:wdt], us[:, wdt:2 * wdt], us[:, 2 * wdt:3 * wdt]
    wd = jnp.tanh(us[:, 3 * wdt:3 * wdt + 128])
    ad = us[:, 3 * wdt + 128:3 * wdt + 256]
    gd = us[:, 3 * wdt + 256:3 * wdt + 384]
    eh = eh_ref[...]
    kkn = k * kk_ref[...]
    kkn = kkn / jnp.maximum(jnp.sqrt(_sel_right(kkn * kkn, eh)), 1e-12)
    bonus_o[0] = _sel_right(r * k * rk_ref[...], eh) * v
    g_o[0] = _mm(_sigmoid(gd), g2_ref[...])
    v_o[0] = v
    blk = RWKV_BLOCK
    for d in range(2):
        wl = w0_ref[d:d + 1, :] + _mm(wd, w2_ref[d])
        e = jnp.exp(_log_sigmoid(wl) - 0.5)
        e_inc = _sel_left(cum_ref[d], e)
        w_inc = jnp.exp(-e_inc)
        inv_w = jnp.exp(e_inc)
        a = _sigmoid(a0_ref[d:d + 1, :] + _mm(ad, a2_ref[d]))
        a1_o[d, 0] = kkn * jnp.exp(e - e_inc)
        a2_o[d, 0] = r * w_inc
        b1_o[d, 0] = kkn * a * inv_w
        b2_o[d, 0] = k * (1.0 + (a - 1.0) * ka_ref[...]) * inv_w
        last = blk - 1 if d == 0 else 0
        for c in range(wdt // LANES):
            cols = slice(c * LANES, (c + 1) * LANES)
            wfull_s[c] = w_inc[:, cols]
            wend_o[d, 0, :, cols] = wfull_s[c, pl.ds(last, tt // blk, stride=blk), :]


def _rwkv_prep(u, p, *, tt):
    bsz, t, _ = u.shape
    wdt = RWKV_WIDTH
    nblk8 = t // SUBLANES
    per = tt // SUBLANES
    c2 = lambda b, j: (0, 0)
    c3 = lambda b, j: (0, 0, 0)
    tok = lambda b, j: (b, j, 0)
    tok2 = lambda b, j: (0, b, j, 0)
    one = jax.ShapeDtypeStruct((bsz, t, wdt), F32)
    two = jax.ShapeDtypeStruct((2, bsz, t, wdt), F32)
    blk = RWKV_BLOCK
    ends = jax.ShapeDtypeStruct((2, bsz, t // blk, wdt), F32)
    ri = jnp.arange(tt)[:, None]
    ci = jnp.arange(tt)[None, :]
    same = (ri // blk) == (ci // blk)
    cum_sel = jnp.stack([same & (ci <= ri), same & (ci >= ri)]).astype(BF16)
    return pl.pallas_call(
        functools.partial(_rwkv_prep_kernel, tt=tt),
        grid=(bsz, t // tt),
        in_specs=[pl.BlockSpec((1, tt, RWKV_IN), tok),
                  pl.BlockSpec((1, SUBLANES, RWKV_IN), lambda b, j: (b, jnp.maximum(j * per - 1, 0), 0)),
                  pl.BlockSpec((1, SUBLANES, RWKV_IN), lambda b, j: (b, jnp.minimum((j + 1) * per, nblk8 - 1), 0)),
                  pl.BlockSpec((1, RWKV_IN), c2),
                  pl.BlockSpec((2, wdt), c2), pl.BlockSpec((2, 128, wdt), c3),
                  pl.BlockSpec((2, wdt), c2), pl.BlockSpec((2, 128, wdt), c3),
                  pl.BlockSpec((128, wdt), c2),
                  pl.BlockSpec((1, wdt), c2), pl.BlockSpec((1, wdt), c2), pl.BlockSpec((1, wdt), c2),
                  pl.BlockSpec((wdt, wdt), c2), pl.BlockSpec((2, tt, tt), c3)],
        out_specs=[pl.BlockSpec((1, tt, wdt), tok)] + [pl.BlockSpec((2, 1, tt, wdt), tok2)] * 4
                  + [pl.BlockSpec((2, 1, tt // blk, wdt), tok2)] + [pl.BlockSpec((1, tt, wdt), tok)] * 2,
        out_shape=[one, two, two, two, two, ends, one, one],
        scratch_shapes=[pltpu.VMEM((wdt // LANES, tt, LANES), F32)],
        compiler_params=_cparams("parallel", "parallel"),
        name="rwkv_prep",
    )(u, u, u, p["mu"], p["w0"], p["w2"], p["a0"], p["a2"], p["g2"], p["k_k"], p["k_a"], p["r_k"], p["eh"], cum_sel)


def _rwkv_scan_kernel(v_ref, a1_ref, a2_ref, b1_ref, b2_ref, wend_ref, s0_ref, y_ref, sT_ref, s_ref, *, k_unroll):
    d = pl.program_id(0)
    j = pl.program_id(1)
    tb = RWKV_BLOCK
    nk = RWKV_HEAD_DIM
    nvb = RWKV_HEAD_DIM // SUBLANES
    lanes = s_ref.shape[1]

    @pl.when(j == 0)
    def _():
        s_ref[...] = s0_ref[...]

    def bcast(ref, t, k):
        return jnp.broadcast_to(ref[t, pl.ds(k, 1), :], (SUBLANES, lanes))

    def srow(k, vb):
        return pl.ds(pl.multiple_of(k * nk + vb * SUBLANES, SUBLANES), SUBLANES)

    def step(i, carry):
        t = jnp.where(d == 0, i, tb - 1 - i)

        def reduce_body(kc, acc):
            acc = list(acc)
            for kq in range(k_unroll):
                k = kc * k_unroll + kq
                a1b = bcast(a1_ref, t, k)
                a2b = bcast(a2_ref, t, k)
                for vb in range(nvb):
                    s = s_ref[srow(k, vb), :]
                    acc[vb] = acc[vb] + s * a1b
                    acc[nvb + vb] = acc[nvb + vb] + s * a2b
            return tuple(acc)

        zero = jnp.zeros((SUBLANES, lanes), F32)
        acc = lax.fori_loop(0, nk // k_unroll, reduce_body, (zero,) * (2 * nvb))
        skk = acc[:nvb]
        a2t = a2_ref[t]
        c1 = jnp.sum(b1_ref[t] * a2t, axis=0, keepdims=True)
        c2 = jnp.sum(b2_ref[t] * a2t, axis=0, keepdims=True)
        vt = [v_ref[t, vb * SUBLANES:(vb + 1) * SUBLANES, :] for vb in range(nvb)]
        for vb in range(nvb):
            y_ref[t, vb * SUBLANES:(vb + 1) * SUBLANES, :] = acc[nvb + vb] - skk[vb] * c1 + vt[vb] * c2

        def update_body(kc, c):
            for kq in range(k_unroll):
                k = kc * k_unroll + kq
                b1b = bcast(b1_ref, t, k)
                b2b = bcast(b2_ref, t, k)
                for vb in range(nvb):
                    idx = srow(k, vb)
                    s_ref[idx, :] = s_ref[idx, :] - skk[vb] * b1b + vt[vb] * b2b
            return c

        lax.fori_loop(0, nk // k_unroll, update_body, 0)
        return carry

    lax.fori_loop(0, tb, step, 0)

    def renorm_body(kc, c):
        for kq in range(k_unroll):
            k = kc * k_unroll + kq
            wb = bcast(wend_ref, 0, k)
            for vb in range(nvb):
                idx = srow(k, vb)
                s_ref[idx, :] = s_ref[idx, :] * wb
        return c

    lax.fori_loop(0, nk // k_unroll, renorm_body, 0)

    @pl.when(j == pl.num_programs(1) - 1)
    def _():
        sT_ref[...] = s_ref[...]


def _rwkv_scan(v, a1, a2, b1, b2, wend, s0):
    t, nk, lanes = v.shape
    tb = RWKV_BLOCK
    nblk = t // tb
    blk_idx = lambda d, j: j + d * (nblk - 1 - 2 * j)
    shared = pl.BlockSpec((tb, nk, lanes), lambda d, j: (blk_idx(d, j), 0, 0))
    perdir = pl.BlockSpec((None, tb, nk, lanes), lambda d, j: (d, blk_idx(d, j), 0, 0))
    ends = pl.BlockSpec((None, 1, nk, lanes), lambda d, j: (d, blk_idx(d, j), 0, 0))
    state = pl.BlockSpec((None, nk * nk, lanes), lambda d, j: (d, 0, 0))
    return pl.pallas_call(
        functools.partial(_rwkv_scan_kernel, k_unroll=16),
        grid=(2, nblk),
        in_specs=[shared, perdir, perdir, perdir, perdir, ends, state],
        out_specs=[perdir, state],
        out_shape=[jax.ShapeDtypeStruct((2, t, nk, lanes), F32), jax.ShapeDtypeStruct((2, nk * nk, lanes), F32)],
        scratch_shapes=[pltpu.VMEM((nk * nk, lanes), F32)],
        compiler_params=_cparams("arbitrary", "arbitrary"),
        name="rwkv_scan",
    )(v, a1, a2, b1, b2, wend, s0)


def _to_time_major(a):
    *lead, bsz, t, _ = a.shape
    nl = len(lead)
    a = a.reshape(*lead, bsz, t, RWKV_HEADS, RWKV_HEAD_DIM)
    a = a.transpose(*range(nl), nl + 1, nl + 3, nl, nl + 2)
    return a.reshape(*lead, t, RWKV_HEAD_DIM, bsz * RWKV_HEADS)


def _from_time_major(a, bsz):
    two, t, n, _ = a.shape
    a = a.reshape(two, t, n, bsz, RWKV_HEADS).transpose(0, 3, 1, 4, 2)
    return a.reshape(two, bsz, t, RWKV_HEADS * n)


def _na_kernel(q_ref, k_ref, v_ref, qc_ref, kc_ref, vc_ref, bias_ref, y_ref, yc_ref, kb_s, vb_s, kcb_s, vcb_s, *, rows):
    scale = NA_HEAD_DIM ** -0.5
    wq = GRID_W
    band = NA_WIN_R * GRID_W
    kb_s[...] = k_ref[0].astype(BF16)
    vb_s[...] = v_ref[0].astype(BF16)
    kcb_s[...] = kc_ref[0].astype(BF16)
    vcb_s[...] = vc_ref[0].astype(BF16)
    low = _iota((1, LANES), 1) < NA_HEAD_DIM

    def stack_heads(qs):
        return jnp.concatenate([jnp.where(low, qs, 0.0), jnp.where(low, 0.0, qs)], axis=0).astype(BF16)

    def softmax_parts(scores):
        m = functools.reduce(jnp.maximum, [jnp.max(s, axis=-1, keepdims=True) for s in scores])
        ps = [jnp.exp(s - m) for s in scores]
        l = functools.reduce(lambda a, b: a + b, [jnp.sum(p, axis=-1, keepdims=True) for p in ps])
        return ps, l

    def combine(ps, l, values):
        o = functools.reduce(lambda a, b: a + b, [_mm(p, v) for p, v in zip(ps, values)]) / l
        m = o.shape[0] // 2
        return jnp.where(low, o[0:m], o[m:2 * m])

    def row_group(i, c):
        jobs = []
        for u in range(NA_ROW_GROUP):
            r = i * NA_ROW_GROUP + u
            rs = jnp.clip(r - NA_WIN_R // 2, 0, rows - NA_WIN_R)
            var = jnp.where(r < NA_WIN_R // 2, r,
                            jnp.where(r > rows - NA_WIN_R // 2, r - (rows - NA_WIN_R), NA_WIN_R // 2))
            q0 = pl.multiple_of(r * wq, wq)
            k0 = pl.multiple_of(rs * wq, wq)
            qst = stack_heads(q_ref[0, pl.ds(q0, wq), :] * scale)
            scores = [_mm_nt(qst, kb_s[pl.ds(k0, band), :]) + bias_ref[0, var], _mm_nt(qst, kcb_s[...])]
            jobs.append((q0, k0, scores))
        soft = [softmax_parts(scores) for _, _, scores in jobs]
        for (q0, k0, _), (ps, l) in zip(jobs, soft):
            y_ref[0, pl.ds(q0, wq), :] = combine(ps, l, [vb_s[pl.ds(k0, band), :], vcb_s[...]])
        return c

    lax.fori_loop(0, rows // NA_ROW_GROUP, row_group, 0)
    ps, l = softmax_parts([_mm_nt(stack_heads(qc_ref[0] * scale), kcb_s[...])])
    yc_ref[0] = combine(ps, l, [vcb_s[...]])


def _na_bias_table(rpb, rows):
    cols = jnp.arange(GRID_W)
    col_start = jnp.clip(cols - NA_WIN_C // 2, 0, GRID_W - NA_WIN_C)
    col_in = (cols[None, :] >= col_start[:, None]) & (cols[None, :] < col_start[:, None] + NA_WIN_C)
    dc_idx = jnp.clip(cols[None, :] - cols[:, None] + NA_WIN_C - 1, 0, 2 * NA_WIN_C - 2)
    rpb_cols = jnp.where(col_in[None, None], rpb[:, :, dc_idx], -jnp.inf)
    half = NA_WIN_R // 2
    rep_rows = list(range(half)) + [half] + list(range(rows - half + 1, rows))
    tiles = []
    for r in rep_rows:
        rs = min(max(r - half, 0), rows - NA_WIN_R)
        dr_idx = rs + jnp.arange(NA_WIN_R) - r + NA_WIN_R - 1
        t = rpb_cols[:, dr_idx]
        tiles.append(t.transpose(0, 2, 1, 3).reshape(NA_HEADS, GRID_W, NA_WIN_R * GRID_W))
    tab = jnp.stack(tiles, axis=1)
    tab = tab.reshape(NA_HEADS // 2, 2, len(rep_rows), GRID_W, NA_WIN_R * GRID_W).transpose(0, 2, 1, 3, 4)
    return tab.reshape(NA_HEADS // 2, len(rep_rows), 2 * GRID_W, NA_WIN_R * GRID_W)


def _na(u, uc, bias_tab, col0):
    bsz, s, _ = u.shape
    lc = uc.shape[1]
    rows = s // GRID_W
    assert rows >= 2 * NA_WIN_R and rows % NA_ROW_GROUP == 0
    qb, kb, vb = col0 // LANES, col0 // LANES + 4, col0 // LANES + 8
    band = NA_WIN_R * GRID_W
    lat = lambda cb: pl.BlockSpec((1, s, LANES), lambda b, p: (b, 0, cb + p))
    cx = lambda cb: pl.BlockSpec((1, lc, LANES), lambda b, p: (b, 0, cb + p))
    return pl.pallas_call(
        functools.partial(_na_kernel, rows=rows),
        grid=(bsz, NA_HEADS // 2),
        in_specs=[lat(qb), lat(kb), lat(vb), cx(qb), cx(kb), cx(vb),
                  pl.BlockSpec((1, NA_WIN_R, 2 * GRID_W, band), lambda b, p: (p, 0, 0, 0))],
        out_specs=[pl.BlockSpec((1, s, LANES), lambda b, p: (b, 0, p)),
                   pl.BlockSpec((1, lc, LANES), lambda b, p: (b, 0, p))],
        out_shape=[jax.ShapeDtypeStruct((bsz, s, NA_HEADS * NA_HEAD_DIM), F32),
                   jax.ShapeDtypeStruct((bsz, lc, NA_HEADS * NA_HEAD_DIM), F32)],
        scratch_shapes=[pltpu.VMEM((s, LANES), BF16), pltpu.VMEM((s, LANES), BF16),
                        pltpu.VMEM((lc, LANES), BF16), pltpu.VMEM((lc, LANES), BF16)],
        compiler_params=_cparams("parallel", "arbitrary"),
        name="na",
    )(u, u, u, uc, uc, uc, bias_tab)


def _gla_kernel(q_ref, k_ref, v_ref, r_ref, gd_ref, kc_ref, vc_ref, gdc_ref, g2_ref, gb_ref, ng_ref, y_ref,
                bc_s, qt_s, kt_s, dec_s, ktc_s, decc_s, st_s, of_s):
    s_len = q_ref.shape[1]
    l_len = kc_ref.shape[1]
    cs = GLA_SUB
    scale = GLA_DK ** -0.5
    ri = _iota((LANES, LANES), 0)
    ci = _iota((LANES, LANES), 1)
    sub_shift = cs.bit_length() - 1
    same_chunk = (ri >> sub_shift) == (ci >> sub_shift)
    lane = _iota((1, LANES), 1)
    head_lane = [lane < GLA_DK, lane >= GLA_DK]
    st_mask = (_iota((2 * GLA_DV, LANES), 0) >= GLA_DV) == (_iota((2 * GLA_DV, LANES), 1) >= GLA_DK)
    trow = _iota((cs, 1), 0)

    cum_sels = [jnp.concatenate([(same_chunk & ((ci <= ri) if d == 0 else (ci >= ri))).astype(F32),
                                 same_chunk.astype(F32)], axis=0).astype(BF16) for d in range(2)]

    def gates(gd):
        gs = [_log_sigmoid(_mm(gd, g2_ref[d, 0]) + gb_ref[d, 0]) / GLA_GATE_NORM for d in range(2)]
        ys = [_sel_left(cum_sels[d], gs[d]) for d in range(2)]
        return [(y[0:LANES], y[LANES:2 * LANES]) for y in ys]

    def prep_lat(i, c):
        rows = pl.ds(pl.multiple_of(i * LANES, LANES), LANES)
        qs = q_ref[0, rows, :] * scale
        kk = k_ref[0, rows, :]
        for d, (bc, tot) in enumerate(gates(gd_ref[0, rows, :])):
            bc_s[d, rows, :] = bc
            qt_s[d, rows, :] = qs * jnp.exp(bc)
            kt_s[d, rows, :] = kk * jnp.exp(tot - bc)
            dec_s[d, rows, :] = jnp.exp(tot)
        return c

    def prep_ctx(i, c):
        rows = pl.ds(pl.multiple_of(i * LANES, LANES), LANES)
        kk = kc_ref[0, rows, :]
        for d, (bc, tot) in enumerate(gates(gdc_ref[0, rows, :])):
            ktc_s[d, rows, :] = kk * jnp.exp(tot - bc)
            decc_s[d, rows, :] = jnp.exp(tot)
        return c

    lax.fori_loop(0, s_len // LANES, prep_lat, 0, unroll=2)
    lax.fori_loop(0, l_len // LANES, prep_ctx, 0, unroll=2)

    for d in range(2):
        fwd = d == 0
        st_s[...] = jnp.zeros(st_s.shape, F32)

        grp = GLA_GROUP
        order = tuple(range(grp)) if fwd else tuple(range(grp - 1, -1, -1))

        def group_starts(i, n_groups, fwd=fwd):
            g = i if fwd else n_groups - 1 - i
            return [(g * grp + u) * cs for u in range(grp)]

        def group_rows(i, n_groups):
            return [pl.ds(pl.multiple_of(r0, cs), cs) for r0 in group_starts(i, n_groups)]

        def increment(kt, vv):
            return jnp.where(st_mask, _mm_tn(vv, kt), 0.0)

        def ctx_group(i, c, order=order):
            rows = group_rows(i, l_len // (cs * grp))
            incs = [increment(ktc_s[d, rows[u], :], vc_ref[0, rows[u], :]) for u in range(grp)]
            st = st_s[...]
            for u in order:
                st = st * decc_s[d, rows[u], :][0:1, :] + incs[u]
            st_s[...] = st
            return c

        def lat_group(i, c, fwd=fwd, order=order):
            starts = group_starts(i, s_len // (cs * grp))
            rows = [pl.ds(pl.multiple_of(r0, cs), cs) for r0 in starts]
            vvs =[v_ref[0, rows[u], :] for u in range(grp)]
            incs = [increment(kt_s[d, rows[u], :], vvs[u]) for u in range(grp)]
            st = st_s[...]
            inter = [None] * grp
            for u in order:
                inter[u] = _mm_nt(qt_s[d, rows[u], :], st)
                st = st * dec_s[d, rows[u], :][0:1, :] + incs[u]
            st_s[...] = st
            rowb = lambda x: jnp.broadcast_to(x, (cs, x.shape[-1]))
            for u in range(grp):
                bc = bc_s[d, rows[u], :]
                qs = q_ref[0, rows[u], :] * scale
                o0, o1 = inter[u][:, 0:GLA_DV], inter[u][:, GLA_DV:2 * GLA_DV]
                for s in range(cs):
                    srow = pl.ds(starts[u] + s, 1)
                    keep = (trow >= s) if fwd else (trow <= s)
                    e = jnp.exp(jnp.where(keep, bc - rowb(bc_s[d, srow, :]), -jnp.inf))
                    term = qs * rowb(k_ref[0, srow, :]) * e
                    a0 = jnp.sum(jnp.where(head_lane[0], term, 0.0), axis=-1, keepdims=True)
                    a1 = jnp.sum(jnp.where(head_lane[1], term, 0.0), axis=-1, keepdims=True)
                    vrow = rowb(v_ref[0, srow, :])
                    o0 = o0 + a0 * vrow[:, 0:GLA_DV]
                    o1 = o1 + a1 * vrow[:, GLA_DV:2 * GLA_DV]
                if fwd:
                    of_s[rows[u], 0:GLA_DV] = o0
                    of_s[rows[u], GLA_DV:2 * GLA_DV] = o1
                else:
                    rr = r_ref[0, rows[u], :]
                    for h, oh in enumerate((o0, o1)):
                        cols = slice(h * GLA_DV, (h + 1) * GLA_DV)
                        ot = of_s[rows[u], cols] + oh
                        on = ot * lax.rsqrt(jnp.mean(ot * ot, axis=-1, keepdims=True) + RMS_EPS) * ng_ref[...]
                        rh = rr[:, cols]
                        y_ref[0, rows[u], cols] = on * (rh * _sigmoid(rh))
            return c

        lax.fori_loop(0, l_len // (cs * grp), ctx_group, 0)
        lax.fori_loop(0, s_len // (cs * grp), lat_group, 0)


def _gla(u, uc, g2p, gbp, norm_g, gd_block):
    bsz, s, _ = u.shape
    lc = uc.shape[1]
    pairs = GLA_HEADS // 2
    dv2 = 2 * GLA_DV
    lat = lambda width, blk: pl.BlockSpec((1, s, width), lambda b, p: (b, 0, blk(p)))
    cx = lambda width, blk: pl.BlockSpec((1, lc, width), lambda b, p: (b, 0, blk(p)))
    q_blk = lambda p: p
    k_blk = lambda p: pairs + p
    v_blk = lambda p: 2 * pairs * LANES // dv2 + p
    r_blk = lambda p: (2 * pairs * LANES + GLA_HEADS * GLA_DV) // dv2 + p
    gd_blk = lambda p: gd_block
    return pl.pallas_call(
        _gla_kernel,
        grid=(bsz, pairs),
        in_specs=[lat(LANES, q_blk), lat(LANES, k_blk), lat(dv2, v_blk), lat(dv2, r_blk), lat(LANES, gd_blk),
                  cx(LANES, k_blk), cx(dv2, v_blk), cx(LANES, gd_blk),
                  pl.BlockSpec((2, 1, LANES, LANES), lambda b, p: (0, p, 0, 0)),
                  pl.BlockSpec((2, 1, 1, LANES), lambda b, p: (0, p, 0, 0)),
                  pl.BlockSpec((1, GLA_DV), lambda b, p: (0, 0))],
        out_specs=pl.BlockSpec((1, s, dv2), lambda b, p: (b, 0, p)),
        out_shape=jax.ShapeDtypeStruct((bsz, s, GLA_HEADS * GLA_DV), F32),
        scratch_shapes=[pltpu.VMEM((2, s, LANES), F32)] * 4 + [pltpu.VMEM((2, lc, LANES), F32)] * 2
                       + [pltpu.VMEM((dv2, LANES), F32), pltpu.VMEM((s, dv2), F32)],
        compiler_params=_cparams("parallel", "arbitrary"),
        name="gla",
    )(u, u, u, u, u, uc, uc, uc, g2p, gbp, norm_g)


def _diff_kernel(q_ref, k_ref, v_ref, kc_ref, vc_ref, lam_ref, ng_ref, y_ref, k_s, v_s, *, lambda_init):
    scale = DIFF_DH ** -0.5
    s_len = k_ref.shape[1]

    @pl.when(pl.program_id(2) == 0)
    def _():
        k_s[0:s_len, :] = k_ref[0].astype(BF16)
        v_s[0:s_len, :] = v_ref[0].astype(BF16)
        k_s[s_len:, :] = kc_ref[0].astype(BF16)
        v_s[s_len:, :] = vc_ref[0].astype(BF16)

    lp = lam_ref[...]
    lam = (jnp.exp(jnp.sum(lp[0:1] * lp[1:2], axis=-1, keepdims=True))
           - jnp.exp(jnp.sum(lp[2:3] * lp[3:4], axis=-1, keepdims=True)) + lambda_init)
    low = _iota((1, LANES), 1) < DIFF_DH
    tq = q_ref.shape[1]
    part = tq // DIFF_Q_PARTS
    scores = []
    for i in range(DIFF_Q_PARTS):
        qs = q_ref[0, i * part:(i + 1) * part, :] * scale
        scores += [_mm_nt(jnp.where(low, qs, 0.0), k_s[...]), _mm_nt(jnp.where(low, 0.0, qs), k_s[...])]
    probs = [jnp.exp(s - jnp.max(s, axis=-1, keepdims=True)) for s in scores]
    outs = [_mm(p, v_s[...]) / jnp.sum(p, axis=-1, keepdims=True) for p in probs]
    for i in range(DIFF_Q_PARTS):
        o = outs[2 * i] - lam * outs[2 * i + 1]
        on = o * lax.rsqrt(jnp.mean(o * o, axis=-1, keepdims=True) + RMS_EPS) * ng_ref[...]
        y_ref[0, i * part:(i + 1) * part, :] = on * (1.0 - lambda_init)


def _diff(u, uc, lam_params, norm_g, col0, lambda_init, *, tq):
    bsz, s, _ = u.shape
    lc = uc.shape[1]
    qb = col0 // LANES
    kb, vb = qb + DIFF_HEADS, qb + 2 * DIFF_HEADS
    full = lambda arr_len, cb: pl.BlockSpec((1, arr_len, LANES), lambda b, h, j: (b, 0, cb + h))
    return pl.pallas_call(
        functools.partial(_diff_kernel, lambda_init=lambda_init),
        grid=(bsz, DIFF_HEADS, s // tq),
        in_specs=[pl.BlockSpec((1, tq, LANES), lambda b, h, j: (b, j, qb + h)),
                  full(s, kb), full(s, vb), full(lc, kb), full(lc, vb),
                  pl.BlockSpec((4, DIFF_DH), lambda b, h, j: (0, 0)),
                  pl.BlockSpec((1, DIFF_DV), lambda b, h, j: (0, 0))],
        out_specs=pl.BlockSpec((1, tq, LANES), lambda b, h, j: (b, j, h)),
        out_shape=jax.ShapeDtypeStruct((bsz, s, DIFF_HEADS * DIFF_DV), F32),
        scratch_shapes=[pltpu.VMEM((s + lc, LANES), BF16), pltpu.VMEM((s + lc, LANES), BF16)],
        compiler_params=_cparams("parallel", "parallel", "arbitrary"),
        name="diff_attn",
    )(u, u, u, uc, uc, lam_params, norm_g)


def _head_selector(width, head_dim, value):
    idx = jnp.arange(width) // head_dim
    return jnp.where(idx[:, None] == idx[None, :], value, 0.0).astype(F32)


def _pad_rows(w, lo, total):
    return jnp.zeros((total, w.shape[1]), w.dtype).at[lo:lo + w.shape[0]].set(w)


def _rope_tables(s, reps):
    pos = jnp.arange(s)
    n = DIFF_DH // 4
    freqs = ROPE_BASE ** (-jnp.arange(n, dtype=F32) / n)
    ang_r = (pos // GRID_W).astype(F32)[:, None] * freqs[None, :]
    ang_c = (pos % GRID_W).astype(F32)[:, None] * freqs[None, :]
    cos = jnp.concatenate([jnp.cos(ang_r)] * 2 + [jnp.cos(ang_c)] * 2, axis=-1)
    sin = jnp.concatenate([-jnp.sin(ang_r), jnp.sin(ang_r), -jnp.sin(ang_c), jnp.sin(ang_c)], axis=-1)
    return jnp.tile(cos, (1, reps)), jnp.tile(sin, (1, reps))


def _rwkv_na_mixer(xl, xc, mods, ctx_row, w_in, mu, w0, w2, a0, a2, g2, k_k, k_a, r_k, gn_g, gn_b, rpb):
    bsz, s, _ = xl.shape
    lc = xc.shape[1]
    n = w_in.shape[1]
    chunks = tuple((lo, min(lo + 512, n)) for lo in range(0, n, 512))
    u = _proj(xl, mods, w_in, chunks, tm=512)
    uc = _proj(xc, mods, w_in, chunks, tm=lc, ctx_row=ctx_row)

    eh = _head_selector(RWKV_WIDTH, RWKV_HEAD_DIM, 1.0).astype(BF16)
    em = _head_selector(RWKV_WIDTH, RWKV_HEAD_DIM, 1.0 / RWKV_HEAD_DIM).astype(BF16)
    p = dict(mu=mu[None], w0=w0, a0=a0, g2=g2.astype(BF16), k_k=k_k[None], k_a=k_a[None], r_k=r_k.reshape(1, -1), eh=eh,
             w2=jnp.stack([_pad_rows(w2[0], 0, 128), _pad_rows(w2[1], 64, 128)]).astype(BF16),
             a2=jnp.stack([_pad_rows(a2[0], 0, 128), _pad_rows(a2[1], 64, 128)]).astype(BF16))
    lanes = bsz * RWKV_HEADS
    state = jnp.zeros((2, RWKV_HEAD_DIM * RWKV_HEAD_DIM, lanes), F32)
    parts = []
    for uu, tt in ((uc, lc), (u, 256)):
        *scan_in, bonus, gate = _rwkv_prep(uu, p, tt=tt)
        y2, state = _rwkv_scan(*[_to_time_major(z) for z in scan_in], state)
        parts.append((_from_time_major(y2, bsz), bonus, gate, gn_g[None], gn_b[None], em))
    yc_a, y_a = parts
    y_b, yc_b = _na(u, uc, _na_bias_table(rpb, s // GRID_W), RWKV_IN)
    return (y_a, y_b), (yc_a, yc_b)


def _gla_diff_mixer(xl, xc, mods, ctx_row, w_in, gla_g2, gla_gb, gla_norm_g, diff_lambda, diff_norm_g, lambda_init):
    bsz, s, d = xl.shape
    lc = xc.shape[1]
    n_gla = 2 * GLA_HEADS * GLA_DK + 2 * GLA_HEADS * GLA_DV
    n_gate = 2 * GLA_GATE_LORA
    n_diff = 2 * DIFF_HEADS * 2 * DIFF_DH + DIFF_HEADS * DIFF_DV
    w_perm = jnp.concatenate([w_in[:, :n_gla], w_in[:, n_gla + n_gate:n_gla + n_gate + n_diff],
                              w_in[:, n_gla:n_gla + n_gate], jnp.zeros((d, LANES - n_gate), w_in.dtype)], axis=1).astype(BF16)
    rope_w = 2 * DIFF_HEADS * 2 * DIFF_DH
    chunks = ((0, 512), (512, 1024), (1024, n_gla), (n_gla, n_gla + rope_w),
              (n_gla + rope_w, n_gla + n_diff), (n_gla + n_diff, n_gla + n_diff + LANES))
    cos_t, sin_t = _rope_tables(s, rope_w // DIFF_DH)
    u = _proj(xl, mods, w_perm, chunks, tm=512, rope=(3, cos_t, sin_t))
    uc = _proj(xc, mods, w_perm, chunks, tm=lc, ctx_row=ctx_row)

    pairs = GLA_HEADS // 2
    g2p = jnp.stack([jnp.stack([_pad_rows(gla_g2[dd][:, p * LANES:(p + 1) * LANES], dd * GLA_GATE_LORA, LANES)
                                for p in range(pairs)]) for dd in range(2)]).astype(BF16)
    gbp = gla_gb.reshape(2, pairs, 1, LANES)
    y_c = _gla(u, uc, g2p, gbp, gla_norm_g[None], (n_gla + n_diff) // LANES)
    y_d = _diff(u, uc, diff_lambda, diff_norm_g[None], n_gla, lambda_init, tq=512)
    return y_c, y_d


def kernel(x, c, ctx, c_ctx, w_mod_0, b_mod_0, ln_g_0, ln_b_0, ffn_gu_0, ffn_down_0, w_in_0, w_out_0, rwkv_mu_0, rwkv_w0_0, rwkv_w2_0, rwkv_a0_0, rwkv_a2_0, rwkv_g2_0, rwkv_k_k_0, rwkv_k_a_0, rwkv_r_k_0, rwkv_gn_g_0, rwkv_gn_b_0, na_rpb_0, w_mod_1, b_mod_1, ln_g_1, ln_b_1, ffn_gu_1, ffn_down_1, w_in_1, w_out_1, gla_g2_1, gla_gb_1, gla_norm_g_1, diff_lambda_1, diff_norm_g_1):
    bsz, s, d = x.shape
    lc = ctx.shape[1]
    ctx_row = bsz
    rows = -(-(bsz + 1) // SUBLANES) * SUBLANES
    c_all = jnp.concatenate([c, c_ctx[None], jnp.zeros((rows - bsz - 1, d), F32)], axis=0)
    tm = 512
    tf = 512

    mods = _mods(c_all, w_mod_0, b_mod_0)
    gu, dn = ffn_gu_0.astype(BF16), ffn_down_0.astype(BF16)
    xl = _ffn(x, mods, 0, gu[0], dn[0], ln_g_0[0], ln_b_0[0], tm=tf)
    xc = _ffn(ctx, mods, 0, gu[0], dn[0], ln_g_0[0], ln_b_0[0], tm=lc, ctx_row=ctx_row)
    (y_a, y_b), (yc_a, yc_b) = _rwkv_na_mixer(xl, xc, mods, ctx_row, w_in_0.astype(BF16), rwkv_mu_0, rwkv_w0_0, rwkv_w2_0,
                                              rwkv_a0_0, rwkv_a2_0, rwkv_g2_0, rwkv_k_k_0, rwkv_k_a_0, rwkv_r_k_0,
                                              rwkv_gn_g_0, rwkv_gn_b_0, na_rpb_0)
    wo = w_out_0.astype(BF16)
    xl = _rwkv_mixout(xl, mods, *y_a, y_b, wo, ln_g_0[1], ln_b_0[1], tm=tm)
    xc = _rwkv_mixout(xc, mods, *yc_a, yc_b, wo, ln_g_0[1], ln_b_0[1], tm=lc, ctx_row=ctx_row)
    xl = _ffn(xl, mods, 6, gu[1], dn[1], ln_g_0[2], ln_b_0[2], tm=tf)
    xc = _ffn(xc, mods, 6, gu[1], dn[1], ln_g_0[2], ln_b_0[2], tm=lc, ctx_row=ctx_row)

    mods = _mods(c_all, w_mod_1, b_mod_1)
    gu, dn = ffn_gu_1.astype(BF16), ffn_down_1.astype(BF16)
    xl = _ffn(xl, mods, 0, gu[0], dn[0], ln_g_1[0], ln_b_1[0], tm=tf)
    xc = _ffn(xc, mods, 0, gu[0], dn[0], ln_g_1[0], ln_b_1[0], tm=lc, ctx_row=ctx_row)
    y_c, y_d = _gla_diff_mixer(xl, xc, mods, ctx_row, w_in_1, gla_g2_1, gla_gb_1, gla_norm_g_1, diff_lambda_1, diff_norm_g_1,
                               0.8 - 0.6 * math.exp(-0.3 * 1))
    xl = _mixout(xl, mods, y_c, y_d, w_out_1.astype(BF16), ln_g_1[1], ln_b_1[1], tm=tm)
    xl = _ffn(xl, mods, 6, gu[1], dn[1], ln_g_1[2], ln_b_1[2], tm=tf)
    return xl
```

```python
import functools
import math

import jax
import jax.numpy as jnp
from jax import lax
from jax.experimental import pallas as pl
from jax.experimental.pallas import tpu as pltpu

F32 = jnp.float32
BF16 = jnp.bfloat16

GRID_W = 64
N_MOD = 9
DEPTH = 2
LN_EPS = 1e-5
RMS_EPS = 1e-6
DEEPNORM_ALPHA = (2 * DEPTH) ** 0.25
RWKV_HEADS = 8
RWKV_HEAD_DIM = 64
RWKV_WIDTH = RWKV_HEADS * RWKV_HEAD_DIM
RWKV_IN = 3 * RWKV_WIDTH + 4 * 64 + 128
RWKV_GN_EPS = 64e-5
RWKV_BLOCK = 32
NA_HEADS = 8
NA_HEAD_DIM = 64
NA_WIN_R = 8
NA_WIN_C = 16
NA_ROW_GROUP = 8
GLA_HEADS = 4
GLA_DK = 64
GLA_DV = 128
GLA_GATE_LORA = 16
GLA_GATE_NORM = 16.0
GLA_SUB = 16
GLA_GROUP = 8
DIFF_HEADS = 4
DIFF_DH = 64
DIFF_DV = 128
DIFF_Q_PARTS = 4
ROPE_BASE = 10000.0

LANES = 128
SUBLANES = 8
V7X_VMEM_BYTES = 64 * 1024 * 1024
VMEM_LIMIT = (V7X_VMEM_BYTES * 7) // 8


def _cparams(*sem):
    return pltpu.CompilerParams(dimension_semantics=sem, vmem_limit_bytes=VMEM_LIMIT)


def _mm(a, b):
    return jnp.dot(a.astype(BF16), b.astype(BF16), preferred_element_type=F32)


def _mm_nt(a, b):
    return lax.dot_general(a.astype(BF16), b.astype(BF16), (((1,), (1,)), ((), ())), preferred_element_type=F32)


def _mm_tn(a, b):
    return lax.dot_general(a.astype(BF16), b.astype(BF16), (((0,), (0,)), ((), ())), preferred_element_type=F32)


def _split3(x):
    hi = x.astype(BF16)
    r1 = x - hi.astype(F32)
    mid = r1.astype(BF16)
    lo = (r1 - mid.astype(F32)).astype(BF16)
    return hi, mid, lo


def _sel_right(x, sel):
    m = x.shape[0]
    y = jnp.dot(jnp.concatenate(_split3(x), axis=0), sel, preferred_element_type=F32)
    return y[0:m] + y[m:2 * m] + y[2 * m:3 * m]


def _sel_left(sel, x):
    n = x.shape[1]
    y = jnp.dot(sel, jnp.concatenate(_split3(x), axis=1), preferred_element_type=F32)
    return y[:, 0:n] + y[:, n:2 * n] + y[:, 2 * n:3 * n]


def _sigmoid(x):
    return 1.0 / (1.0 + jnp.exp(-x))


def _log_sigmoid(x):
    return -(jnp.maximum(-x, 0.0) + jnp.log(1.0 + jnp.exp(-jnp.abs(x))))


def _layer_norm(z, g, b):
    mu = jnp.mean(z, axis=-1, keepdims=True)
    d = z - mu
    var = jnp.mean(d * d, axis=-1, keepdims=True)
    return d * lax.rsqrt(var + LN_EPS) * g + b


def _iota(shape, dim):
    return lax.broadcasted_iota(jnp.int32, shape, dim)


def _mods_kernel(c_ref, w_ref, b_ref, o_ref):
    c = c_ref[...]
    o_ref[...] = _mm(c * _sigmoid(c), w_ref[...]) + b_ref[...]


def _mods(c_all, w_mod, b_mod):
    rows, d = c_all.shape
    n = w_mod.shape[1]
    tn = n // 8
    out = pl.pallas_call(
        _mods_kernel,
        grid=(n // tn,),
        in_specs=[pl.BlockSpec((rows, d), lambda j: (0, 0)),
                  pl.BlockSpec((d, tn), lambda j: (0, j)),
                  pl.BlockSpec((1, tn), lambda j: (0, j))],
        out_specs=pl.BlockSpec((rows, tn), lambda j: (0, j)),
        out_shape=jax.ShapeDtypeStruct((rows, n), F32),
        compiler_params=_cparams("arbitrary"),
        name="mods",
    )(c_all, w_mod, b_mod[None])
    return out.reshape(rows, N_MOD, d)


def _mod_spec(d, ctx_row):
    if ctx_row is None:
        return pl.BlockSpec((1, N_MOD, d), lambda b, j: (b, 0, 0))
    return pl.BlockSpec((1, N_MOD, d), lambda b, j: (ctx_row, 0, 0))


def _ffn_kernel(x_ref, m_ref, wgu_ref, wd_ref, g_ref, b_ref, o_ref, *, i0, fc):
    hidden = wd_ref.shape[0]
    x = x_ref[0]
    shift, scale, gate = (m_ref[0, i0 + i:i0 + i + 1, :] for i in range(3))
    h = (x * (1.0 + scale) + shift).astype(BF16)
    acc = None
    for c in range(hidden // fc):
        g = jnp.dot(h, wgu_ref[:, c * fc:(c + 1) * fc], preferred_element_type=F32)
        u = jnp.dot(h, wgu_ref[:, hidden + c * fc:hidden + (c + 1) * fc], preferred_element_type=F32)
        a = (g * _sigmoid(g) * u).astype(BF16)
        dn = jnp.dot(a, wd_ref[c * fc:(c + 1) * fc, :], preferred_element_type=F32)
        acc = dn if acc is None else acc + dn
    z = DEEPNORM_ALPHA * x + gate * (0.5 * acc)
    o_ref[0] = _layer_norm(z, g_ref[...], b_ref[...])


def _ffn(x, mods, i0, w_gu, w_down, ln_g, ln_b, *, tm, ctx_row=None):
    bsz, t, d = x.shape
    hidden = w_down.shape[0]
    const = lambda b, j: (0, 0)
    return pl.pallas_call(
        functools.partial(_ffn_kernel, i0=i0, fc=256),
        grid=(bsz, t // tm),
        in_specs=[pl.BlockSpec((1, tm, d), lambda b, j: (b, j, 0)),
                  _mod_spec(d, ctx_row),
                  pl.BlockSpec((d, 2 * hidden), const, pipeline_mode=pl.Buffered(1)),
                  pl.BlockSpec((hidden, d), const, pipeline_mode=pl.Buffered(1)),
                  pl.BlockSpec((1, d), const),
                  pl.BlockSpec((1, d), const)],
        out_specs=pl.BlockSpec((1, tm, d), lambda b, j: (b, j, 0)),
        out_shape=jax.ShapeDtypeStruct(x.shape, F32),
        compiler_params=_cparams("parallel", "parallel"),
        name="ffn",
    )(x, mods, w_gu, w_down, ln_g[None], ln_b[None])


def _proj_kernel(x_ref, m_ref, w_ref, *rest, chunks, rope_chunk):
    if rope_chunk is None:
        (o_ref,) = rest
    else:
        cos_ref, sin_ref, o_ref = rest
    x = x_ref[0]
    h = (x * (1.0 + m_ref[0, 4:5, :]) + m_ref[0, 3:4, :]).astype(BF16)
    for ci, (lo, hi) in enumerate(chunks):
        u = jnp.dot(h, w_ref[:, lo:hi], preferred_element_type=F32)
        if ci == rope_chunk:
            n = hi - lo
            first = (_iota((1, n), 1) & 16) == 0
            partner = jnp.where(first, pltpu.roll(u, n - 16, axis=1), pltpu.roll(u, 16, axis=1))
            u = u * cos_ref[...] + partner * sin_ref[...]
        o_ref[0, :, lo:hi] = u


def _proj(x, mods, w_in, chunks, *, tm, ctx_row=None, rope=None):
    bsz, t, d = x.shape
    n = w_in.shape[1]
    const = lambda b, j: (0, 0)
    in_specs = [pl.BlockSpec((1, tm, d), lambda b, j: (b, j, 0)),
                _mod_spec(d, ctx_row),
                pl.BlockSpec((d, n), const, pipeline_mode=pl.Buffered(1))]
    args = [x, mods, w_in]
    rope_chunk = None
    if rope is not None:
        rope_chunk, cos_t, sin_t = rope
        width = cos_t.shape[1]
        in_specs += [pl.BlockSpec((tm, width), lambda b, j: (j, 0))] * 2
        args += [cos_t, sin_t]
    return pl.pallas_call(
        functools.partial(_proj_kernel, chunks=chunks, rope_chunk=rope_chunk),
        grid=(bsz, t // tm),
        in_specs=in_specs,
        out_specs=pl.BlockSpec((1, tm, n), lambda b, j: (b, j, 0)),
        out_shape=jax.ShapeDtypeStruct((bsz, t, n), F32),
        compiler_params=_cparams("parallel", "parallel"),
        name="proj",
    )(*args)


def _mixout_kernel(x_ref, m_ref, ya_ref, yb_ref, w_ref, g_ref, b_ref, o_ref):
    x = x_ref[0]
    half = ya_ref.shape[2]
    y = _mm(ya_ref[0], w_ref[0:half, :]) + _mm(yb_ref[0], w_ref[half:2 * half, :])
    z = DEEPNORM_ALPHA * x + m_ref[0, 5:6, :] * y
    o_ref[0] = _layer_norm(z, g_ref[...], b_ref[...])


def _rwkv_mixout_kernel(x_ref, m_ref, yf_ref, yr_ref, bonus_ref, gate_ref, gng_ref, gnb_ref, em_ref, yb_ref, w_ref,
                        g_ref, b_ref, o_ref):
    x = x_ref[0]
    half = yb_ref.shape[2]
    y = yf_ref[0, 0] + yr_ref[0, 0]
    em = em_ref[...]
    dlt = y - _sel_right(y, em)
    var = _sel_right(dlt * dlt, em)
    ya = (dlt * lax.rsqrt(var + RWKV_GN_EPS) * gng_ref[...] + gnb_ref[...] + bonus_ref[0]) * gate_ref[0]
    y = _mm(ya, w_ref[0:half, :]) + _mm(yb_ref[0], w_ref[half:2 * half, :])
    z = DEEPNORM_ALPHA * x + m_ref[0, 5:6, :] * y
    o_ref[0] = _layer_norm(z, g_ref[...], b_ref[...])


def _rwkv_mixout(x, mods, y2, bonus, gate, gn_g, gn_b, em, yb, w_out, ln_g, ln_b, *, tm, ctx_row=None):
    bsz, t, d = x.shape
    half = yb.shape[2]
    const = lambda b, j: (0, 0)
    tok = lambda b, j: (b, j, 0)
    head = pl.BlockSpec((1, tm, half), tok)
    return pl.pallas_call(
        _rwkv_mixout_kernel,
        grid=(bsz, t // tm),
        in_specs=[pl.BlockSpec((1, tm, d), tok), _mod_spec(d, ctx_row),
                  pl.BlockSpec((1, 1, tm, half), lambda b, j: (0, b, j, 0)),
                  pl.BlockSpec((1, 1, tm, half), lambda b, j: (1, b, j, 0)),
                  head, head, pl.BlockSpec((1, half), const), pl.BlockSpec((1, half), const),
                  pl.BlockSpec((half, half), const), head,
                  pl.BlockSpec((2 * half, d), const, pipeline_mode=pl.Buffered(1)),
                  pl.BlockSpec((1, d), const), pl.BlockSpec((1, d), const)],
        out_specs=pl.BlockSpec((1, tm, d), tok),
        out_shape=jax.ShapeDtypeStruct(x.shape, F32),
        compiler_params=_cparams("parallel", "parallel"),
        name="rwkv_mixout",
    )(x, mods, y2, y2, bonus, gate, gn_g, gn_b, em, yb, w_out, ln_g[None], ln_b[None])


def _mixout(x, mods, ya, yb, w_out, ln_g, ln_b, *, tm, ctx_row=None):
    bsz, t, d = x.shape
    half = ya.shape[2]
    const = lambda b, j: (0, 0)
    tok = lambda b, j: (b, j, 0)
    return pl.pallas_call(
        _mixout_kernel,
        grid=(bsz, t // tm),
        in_specs=[pl.BlockSpec((1, tm, d), tok), _mod_spec(d, ctx_row),
                  pl.BlockSpec((1, tm, half), tok), pl.BlockSpec((1, tm, half), tok),
                  pl.BlockSpec((2 * half, d), const, pipeline_mode=pl.Buffered(1)),
                  pl.BlockSpec((1, d), const), pl.BlockSpec((1, d), const)],
        out_specs=pl.BlockSpec((1, tm, d), tok),
        out_shape=jax.ShapeDtypeStruct(x.shape, F32),
        compiler_params=_cparams("parallel", "parallel"),
        name="mixout",
    )(x, mods, ya, yb, w_out, ln_g[None], ln_b[None])


def _rwkv_prep_kernel(u_ref, up_ref, un_ref, mu_ref, w0_ref, w2_ref, a0_ref, a2_ref, g2_ref, kk_ref, ka_ref, rk_ref,
                      eh_ref, cum_ref, v_o, a1_o, a2_o, b1_o, b2_o, wend_o, bonus_o, g_o, wfull_s, *, tt):
    j = pl.program_id(1)
    nj = pl.num_programs(1)
    wdt = RWKV_WIDTH
    u = u_ref[0]
    prev_row = up_ref[0, SUBLANES - 1:SUBLANES, :] * (j > 0).astype(F32)
    next_row = un_ref[0, 0:1, :] * (j < nj - 1).astype(F32)
    row = _iota((tt, 1), 0)
    up = jnp.where(row == 0, prev_row, pltpu.roll(u, 1, axis=0))
    un = jnp.where(row == tt - 1, next_row, pltpu.roll(u, tt - 1, axis=0))
    us = u + mu_ref[...] * (0.5 * (up + un) - u)
    r, k, v = us[:, 0:wdt], us[:, wdt:2 * wdt], us[:, 2 * wdt:3 * wdt]
    wd = jnp.tanh(us[:, 3 * wdt:3 * wdt + 128])
    ad = us[:, 3 * wdt + 128:3 * wdt + 256]
    gd = us[:, 3 * wdt + 256:3 * wdt + 384]
    eh = eh_ref[...]
    kkn = k * kk_ref[...]
    kkn = kkn / jnp.maximum(jnp.sqrt(_sel_right(kkn * kkn, eh)), 1e-12)
    bonus_o[0] = _sel_right(r * k * rk_ref[...], eh) * v
    g_o[0] = _mm(_sigmoid(gd), g2_ref[...])
    v_o[0] = v
    blk = RWKV_BLOCK
    for d in range(2):
        wl = w0_ref[d:d + 1, :] + _mm(wd, w2_ref[d])
        e = jnp.exp(_log_sigmoid(wl) - 0.5)
        e_inc = _sel_left(cum_ref[d], e)
        w_inc = jnp.exp(-e_inc)
        inv_w = jnp.exp(e_inc)
        a = _sigmoid(a0_ref[d:d + 1, :] + _mm(ad, a2_ref[d]))
        a1_o[d, 0] = kkn * jnp.exp(e - e_inc)
        a2_o[d, 0] = r * w_inc
        b1_o[d, 0] = kkn * a * inv_w
        b2_o[d, 0] = k * (1.0 + (a - 1.0) * ka_ref[...]) * inv_w
        last = blk - 1 if d == 0 else 0
        for c in range(wdt // LANES):
            cols = slice(c * LANES, (c + 1) * LANES)
            wfull_s[c] = w_inc[:, cols]
            wend_o[d, 0, :, cols] = wfull_s[c, pl.ds(last, tt // blk, stride=blk), :]


def _rwkv_prep(u, p, *, tt):
    bsz, t, _ = u.shape
    wdt = RWKV_WIDTH
    nblk8 = t // SUBLANES
    per = tt // SUBLANES
    c2 = lambda b, j: (0, 0)
    c3 = lambda b, j: (0, 0, 0)
    tok = lambda b, j: (b, j, 0)
    tok2 = lambda b, j: (0, b, j, 0)
    one = jax.ShapeDtypeStruct((bsz, t, wdt), F32)
    two = jax.ShapeDtypeStruct((2, bsz, t, wdt), F32)
    blk = RWKV_BLOCK
    ends = jax.ShapeDtypeStruct((2, bsz, t // blk, wdt), F32)
    ri = jnp.arange(tt)[:, None]
    ci = jnp.arange(tt)[None, :]
    same = (ri // blk) == (ci // blk)
    cum_sel = jnp.stack([same & (ci <= ri), same & (ci >= ri)]).astype(BF16)
    return pl.pallas_call(
        functools.partial(_rwkv_prep_kernel, tt=tt),
        grid=(bsz, t // tt),
        in_specs=[pl.BlockSpec((1, tt, RWKV_IN), tok),
                  pl.BlockSpec((1, SUBLANES, RWKV_IN), lambda b, j: (b, jnp.maximum(j * per - 1, 0), 0)),
                  pl.BlockSpec((1, SUBLANES, RWKV_IN), lambda b, j: (b, jnp.minimum((j + 1) * per, nblk8 - 1), 0)),
                  pl.BlockSpec((1, RWKV_IN), c2),
                  pl.BlockSpec((2, wdt), c2), pl.BlockSpec((2, 128, wdt), c3),
                  pl.BlockSpec((2, wdt), c2), pl.BlockSpec((2, 128, wdt), c3),
                  pl.BlockSpec((128, wdt), c2),
                  pl.BlockSpec((1, wdt), c2), pl.BlockSpec((1, wdt), c2), pl.BlockSpec((1, wdt), c2),
                  pl.BlockSpec((wdt, wdt), c2), pl.BlockSpec((2, tt, tt), c3)],
        out_specs=[pl.BlockSpec((1, tt, wdt), tok)] + [pl.BlockSpec((2, 1, tt, wdt), tok2)] * 4
                  + [pl.BlockSpec((2, 1, tt // blk, wdt), tok2)] + [pl.BlockSpec((1, tt, wdt), tok)] * 2,
        out_shape=[one, two, two, two, two, ends, one, one],
        scratch_shapes=[pltpu.VMEM((wdt // LANES, tt, LANES), F32)],
        compiler_params=_cparams("parallel", "parallel"),
        name="rwkv_prep",
    )(u, u, u, p["mu"], p["w0"], p["w2"], p["a0"], p["a2"], p["g2"], p["k_k"], p["k_a"], p["r_k"], p["eh"], cum_sel)


def _rwkv_scan_kernel(v_ref, a1_ref, a2_ref, b1_ref, b2_ref, wend_ref, s0_ref, y_ref, sT_ref, s_ref, *, k_unroll):
    d = pl.program_id(0)
    j = pl.program_id(1)
    tb = RWKV_BLOCK
    nk = RWKV_HEAD_DIM
    nvb = RWKV_HEAD_DIM // SUBLANES
    lanes = s_ref.shape[1]

    @pl.when(j == 0)
    def _():
        s_ref[...] = s0_ref[...]

    def bcast(ref, t, k):
        return jnp.broadcast_to(ref[t, pl.ds(k, 1), :], (SUBLANES, lanes))

    def srow(k, vb):
        return pl.ds(pl.multiple_of(k * nk + vb * SUBLANES, SUBLANES), SUBLANES)

    def step(i, carry):
        t = jnp.where(d == 0, i, tb - 1 - i)

        def reduce_body(kc, acc):
            acc = list(acc)
            for kq in range(k_unroll):
                k = kc * k_unroll + kq
                a1b = bcast(a1_ref, t, k)
                a2b = bcast(a2_ref, t, k)
                for vb in range(nvb):
                    s = s_ref[srow(k, vb), :]
                    acc[vb] = acc[vb] + s * a1b
                    acc[nvb + vb] = acc[nvb + vb] + s * a2b
            return tuple(acc)

        zero = jnp.zeros((SUBLANES, lanes), F32)
        acc = lax.fori_loop(0, nk // k_unroll, reduce_body, (zero,) * (2 * nvb))
        skk = acc[:nvb]
        a2t = a2_ref[t]
        c1 = jnp.sum(b1_ref[t] * a2t, axis=0, keepdims=True)
        c2 = jnp.sum(b2_ref[t] * a2t, axis=0, keepdims=True)
        vt = [v_ref[t, vb * SUBLANES:(vb + 1) * SUBLANES, :] for vb in range(nvb)]
        for vb in range(nvb):
            y_ref[t, vb * SUBLANES:(vb + 1) * SUBLANES, :] = acc[nvb + vb] - skk[vb] * c1 + vt[vb] * c2

        def update_body(kc, c):
            for kq in range(k_unroll):
                k = kc * k_unroll + kq
                b1b = bcast(b1_ref, t, k)
                b2b = bcast(b2_ref, t, k)
                for vb in range(nvb):
                    idx = srow(k, vb)
                    s_ref[idx, :] = s_ref[idx, :] - skk[vb] * b1b + vt[vb] * b2b
            return c

        lax.fori_loop(0, nk // k_unroll, update_body, 0)
        return carry

    lax.fori_loop(0, tb, step, 0)

    def renorm_body(kc, c):
        for kq in range(k_unroll):
            k = kc * k_unroll + kq
            wb = bcast(wend_ref, 0, k)
            for vb in range(nvb):
                idx = srow(k, vb)
                s_ref[idx, :] = s_ref[idx, :] * wb
        return c

    lax.fori_loop(0, nk // k_unroll, renorm_body, 0)

    @pl.when(j == pl.num_programs(1) - 1)
    def _():
        sT_ref[...] = s_ref[...]


def _rwkv_scan(v, a1, a2, b1, b2, wend, s0):
    t, nk, lanes = v.shape
    tb = RWKV_BLOCK
    nblk = t // tb
    blk_idx = lambda d, j: j + d * (nblk - 1 - 2 * j)
    shared = pl.BlockSpec((tb, nk, lanes), lambda d, j: (blk_idx(d, j), 0, 0))
    perdir = pl.BlockSpec((None, tb, nk, lanes), lambda d, j: (d, blk_idx(d, j), 0, 0))
    ends = pl.BlockSpec((None, 1, nk, lanes), lambda d, j: (d, blk_idx(d, j), 0, 0))
    state = pl.BlockSpec((None, nk * nk, lanes), lambda d, j: (d, 0, 0))
    return pl.pallas_call(
        functools.partial(_rwkv_scan_kernel, k_unroll=16),
        grid=(2, nblk),
        in_specs=[shared, perdir, perdir, perdir, perdir, ends, state],
        out_specs=[perdir, state],
        out_shape=[jax.ShapeDtypeStruct((2, t, nk, lanes), F32), jax.ShapeDtypeStruct((2, nk * nk, lanes), F32)],
        scratch_shapes=[pltpu.VMEM((nk * nk, lanes), F32)],
        compiler_params=_cparams("arbitrary", "arbitrary"),
        name="rwkv_scan",
    )(v, a1, a2, b1, b2, wend, s0)


def _to_time_major(a):
    *lead, bsz, t, _ = a.shape
    nl = len(lead)
    a = a.reshape(*lead, bsz, t, RWKV_HEADS, RWKV_HEAD_DIM)
    a = a.transpose(*range(nl), nl + 1, nl + 3, nl, nl + 2)
    return a.reshape(*lead, t, RWKV_HEAD_DIM, bsz * RWKV_HEADS)


def _from_time_major(a, bsz):
    two, t, n, _ = a.shape
    a = a.reshape(two, t, n, bsz, RWKV_HEADS).transpose(0, 3, 1, 4, 2)
    return a.reshape(two, bsz, t, RWKV_HEADS * n)


def _na_kernel(q_ref, k_ref, v_ref, qc_ref, kc_ref, vc_ref, bias_ref, y_ref, yc_ref, kb_s, vb_s, kcb_s, vcb_s, *, rows):
    scale = NA_HEAD_DIM ** -0.5
    wq = GRID_W
    band = NA_WIN_R * GRID_W
    kb_s[...] = k_ref[0].astype(BF16)
    vb_s[...] = v_ref[0].astype(BF16)
    kcb_s[...] = kc_ref[0].astype(BF16)
    vcb_s[...] = vc_ref[0].astype(BF16)
    low = _iota((1, LANES), 1) < NA_HEAD_DIM

    def stack_heads(qs):
        return jnp.concatenate([jnp.where(low, qs, 0.0), jnp.where(low, 0.0, qs)], axis=0).astype(BF16)

    def softmax_parts(scores):
        m = functools.reduce(jnp.maximum, [jnp.max(s, axis=-1, keepdims=True) for s in scores])
        ps = [jnp.exp(s - m) for s in scores]
        l = functools.reduce(lambda a, b: a + b, [jnp.sum(p, axis=-1, keepdims=True) for p in ps])
        return ps, l

    def combine(ps, l, values):
        o = functools.reduce(lambda a, b: a + b, [_mm(p, v) for p, v in zip(ps, values)]) / l
        m = o.shape[0] // 2
        return jnp.where(low, o[0:m], o[m:2 * m])

    def row_group(i, c):
        jobs = []
        for u in range(NA_ROW_GROUP):
            r = i * NA_ROW_GROUP + u
            rs = jnp.clip(r - NA_WIN_R // 2, 0, rows - NA_WIN_R)
            var = jnp.where(r < NA_WIN_R // 2, r,
                            jnp.where(r > rows - NA_WIN_R // 2, r - (rows - NA_WIN_R), NA_WIN_R // 2))
            q0 = pl.multiple_of(r * wq, wq)
            k0 = pl.multiple_of(rs * wq, wq)
            qst = stack_heads(q_ref[0, pl.ds(q0, wq), :] * scale)
            scores = [_mm_nt(qst, kb_s[pl.ds(k0, band), :]) + bias_ref[0, var], _mm_nt(qst, kcb_s[...])]
            jobs.append((q0, k0, scores))
        soft = [softmax_parts(scores) for _, _, scores in jobs]
        for (q0, k0, _), (ps, l) in zip(jobs, soft):
            y_ref[0, pl.ds(q0, wq), :] = combine(ps, l, [vb_s[pl.ds(k0, band), :], vcb_s[...]])
        return c

    lax.fori_loop(0, rows // NA_ROW_GROUP, row_group, 0)
    ps, l = softmax_parts([_mm_nt(stack_heads(qc_ref[0] * scale), kcb_s[...])])
    yc_ref[0] = combine(ps, l, [vcb_s[...]])


def _na_bias_table(rpb, rows):
    cols = jnp.arange(GRID_W)
    col_start = jnp.clip(cols - NA_WIN_C // 2, 0, GRID_W - NA_WIN_C)
    col_in = (cols[None, :] >= col_start[:, None]) & (cols[None, :] < col_start[:, None] + NA_WIN_C)
    dc_idx = jnp.clip(cols[None, :] - cols[:, None] + NA_WIN_C - 1, 0, 2 * NA_WIN_C - 2)
    rpb_cols = jnp.where(col_in[None, None], rpb[:, :, dc_idx], -jnp.inf)
    half = NA_WIN_R // 2
    rep_rows = list(range(half)) + [half] + list(range(rows - half + 1, rows))
    tiles = []
    for r in rep_rows:
        rs = min(max(r - half, 0), rows - NA_WIN_R)
        dr_idx = rs + jnp.arange(NA_WIN_R) - r + NA_WIN_R - 1
        t = rpb_cols[:, dr_idx]
        tiles.append(t.transpose(0, 2, 1, 3).reshape(NA_HEADS, GRID_W, NA_WIN_R * GRID_W))
    tab = jnp.stack(tiles, axis=1)
    tab = tab.reshape(NA_HEADS // 2, 2, len(rep_rows), GRID_W, NA_WIN_R * GRID_W).transpose(0, 2, 1, 3, 4)
    return tab.reshape(NA_HEADS // 2, len(rep_rows), 2 * GRID_W, NA_WIN_R * GRID_W)


def _na(u, uc, bias_tab, col0):
    bsz, s, _ = u.shape
    lc = uc.shape[1]
    rows = s // GRID_W
    assert rows >= 2 * NA_WIN_R and rows % NA_ROW_GROUP == 0
    qb, kb, vb = col0 // LANES, col0 // LANES + 4, col0 // LANES + 8
    band = NA_WIN_R * GRID_W
    lat = lambda cb: pl.BlockSpec((1, s, LANES), lambda b, p: (b, 0, cb + p))
    cx = lambda cb: pl.BlockSpec((1, lc, LANES), lambda b, p: (b, 0, cb + p))
    return pl.pallas_call(
        functools.partial(_na_kernel, rows=rows),
        grid=(bsz, NA_HEADS // 2),
        in_specs=[lat(qb), lat(kb), lat(vb), cx(qb), cx(kb), cx(vb),
                  pl.BlockSpec((1, NA_WIN_R, 2 * GRID_W, band), lambda b, p: (p, 0, 0, 0))],
        out_specs=[pl.BlockSpec((1, s, LANES), lambda b, p: (b, 0, p)),
                   pl.BlockSpec((1, lc, LANES), lambda b, p: (b, 0, p))],
        out_shape=[jax.ShapeDtypeStruct((bsz, s, NA_HEADS * NA_HEAD_DIM), F32),
                   jax.ShapeDtypeStruct((bsz, lc, NA_HEADS * NA_HEAD_DIM), F32)],
        scratch_shapes=[pltpu.VMEM((s, LANES), BF16), pltpu.VMEM((s, LANES), BF16),
                        pltpu.VMEM((lc, LANES), BF16), pltpu.VMEM((lc, LANES), BF16)],
        compiler_params=_cparams("parallel", "arbitrary"),
        name="na",
    )(u, u, u, uc, uc, uc, bias_tab)


def _gla_kernel(q_ref, k_ref, v_ref, r_ref, gd_ref, kc_ref, vc_ref, gdc_ref, g2_ref, gb_ref, ng_ref, y_ref,
                bc_s, qt_s, kt_s, dec_s, ktc_s, decc_s, st_s, of_s):
    s_len = q_ref.shape[1]
    l_len = kc_ref.shape[1]
    cs = GLA_SUB
    scale = GLA_DK ** -0.5
    ri = _iota((LANES, LANES), 0)
    ci = _iota((LANES, LANES), 1)
    sub_shift = cs.bit_length() - 1
    same_chunk = (ri >> sub_shift) == (ci >> sub_shift)
    lane = _iota((1, LANES), 1)
    head_lane = [lane < GLA_DK, lane >= GLA_DK]
    st_mask = (_iota((2 * GLA_DV, LANES), 0) >= GLA_DV) == (_iota((2 * GLA_DV, LANES), 1) >= GLA_DK)
    trow = _iota((cs, 1), 0)

    cum_sels = [jnp.concatenate([(same_chunk & ((ci <= ri) if d == 0 else (ci >= ri))).astype(F32),
                                 same_chunk.astype(F32)], axis=0).astype(BF16) for d in range(2)]

    def gates(gd):
        gs = [_log_sigmoid(_mm(gd, g2_ref[d, 0]) + gb_ref[d, 0]) / GLA_GATE_NORM for d in range(2)]
        ys = [_sel_left(cum_sels[d], gs[d]) for d in range(2)]
        return [(y[0:LANES], y[LANES:2 * LANES]) for y in ys]

    def prep_lat(i, c):
        rows = pl.ds(pl.multiple_of(i * LANES, LANES), LANES)
        qs = q_ref[0, rows, :] * scale
        kk = k_ref[0, rows, :]
        for d, (bc, tot) in enumerate(gates(gd_ref[0, rows, :])):
            bc_s[d, rows, :] = bc
            qt_s[d, rows, :] = qs * jnp.exp(bc)
            kt_s[d, rows, :] = kk * jnp.exp(tot - bc)
            dec_s[d, rows, :] = jnp.exp(tot)
        return c

    def prep_ctx(i, c):
        rows = pl.ds(pl.multiple_of(i * LANES, LANES), LANES)
        kk = kc_ref[0, rows, :]
        for d, (bc, tot) in enumerate(gates(gdc_ref[0, rows, :])):
            ktc_s[d, rows, :] = kk * jnp.exp(tot - bc)
            decc_s[d, rows, :] = jnp.exp(tot)
        return c

    lax.fori_loop(0, s_len // LANES, prep_lat, 0, unroll=2)
    lax.fori_loop(0, l_len // LANES, prep_ctx, 0, unroll=2)

    for d in range(2):
        fwd = d == 0
        st_s[...] = jnp.zeros(st_s.shape, F32)

        grp = GLA_GROUP
        order = tuple(range(grp)) if fwd else tuple(range(grp - 1, -1, -1))

        def group_starts(i, n_groups, fwd=fwd):
            g = i if fwd else n_groups - 1 - i
            return [(g * grp + u) * cs for u in range(grp)]

        def group_rows(i, n_groups):
            return [pl.ds(pl.multiple_of(r0, cs), cs) for r0 in group_starts(i, n_groups)]

        def increment(kt, vv):
            return jnp.where(st_mask, _mm_tn(vv, kt), 0.0)

        def ctx_group(i, c, order=order):
            rows = group_rows(i, l_len // (cs * grp))
            incs = [increment(ktc_s[d, rows[u], :], vc_ref[0, rows[u], :]) for u in range(grp)]
            st = st_s[...]
            for u in order:
                st = st * decc_s[d, rows[u], :][0:1, :] + incs[u]
            st_s[...] = st
            return c

        def lat_group(i, c, fwd=fwd, order=order):
            starts = group_starts(i, s_len // (cs * grp))
            rows = [pl.ds(pl.multiple_of(r0, cs), cs) for r0 in starts]
            vvs =[v_ref[0, rows[u], :] for u in range(grp)]
            incs = [increment(kt_s[d, rows[u], :], vvs[u]) for u in range(grp)]
            st = st_s[...]
            inter = [None] * grp
            for u in order:
                inter[u] = _mm_nt(qt_s[d, rows[u], :], st)
                st = st * dec_s[d, rows[u], :][0:1, :] + incs[u]
            st_s[...] = st
            rowb = lambda x: jnp.broadcast_to(x, (cs, x.shape[-1]))
            for u in range(grp):
                bc = bc_s[d, rows[u], :]
                qs = q_ref[0, rows[u], :] * scale
                o0, o1 = inter[u][:, 0:GLA_DV], inter[u][:, GLA_DV:2 * GLA_DV]
                for s in range(cs):
                    srow = pl.ds(starts[u] + s, 1)
                    keep = (trow >= s) if fwd else (trow <= s)
                    e = jnp.exp(jnp.where(keep, bc - rowb(bc_s[d, srow, :]), -jnp.inf))
                    term = qs * rowb(k_ref[0, srow, :]) * e
                    a0 = jnp.sum(jnp.where(head_lane[0], term, 0.0), axis=-1, keepdims=True)
                    a1 = jnp.sum(jnp.where(head_lane[1], term, 0.0), axis=-1, keepdims=True)
                    vrow = rowb(v_ref[0, srow, :])
                    o0 = o0 + a0 * vrow[:, 0:GLA_DV]
                    o1 = o1 + a1 * vrow[:, GLA_DV:2 * GLA_DV]
                if fwd:
                    of_s[rows[u], 0:GLA_DV] = o0
                    of_s[rows[u], GLA_DV:2 * GLA_DV] = o1
                else:
                    rr = r_ref[0, rows[u], :]
                    for h, oh in enumerate((o0, o1)):
                        cols = slice(h * GLA_DV, (h + 1) * GLA_DV)
                        ot = of_s[rows[u], cols] + oh
                        on = ot * lax.rsqrt(jnp.mean(ot * ot, axis=-1, keepdims=True) + RMS_EPS) * ng_ref[...]
                        rh = rr[:, cols]
                        y_ref[0, rows[u], cols] = on * (rh * _sigmoid(rh))
            return c

        lax.fori_loop(0, l_len // (cs * grp), ctx_group, 0)
        lax.fori_loop(0, s_len // (cs * grp), lat_group, 0)


def _gla(u, uc, g2p, gbp, norm_g, gd_block):
    bsz, s, _ = u.shape
    lc = uc.shape[1]
    pairs = GLA_HEADS // 2
    dv2 = 2 * GLA_DV
    lat = lambda width, blk: pl.BlockSpec((1, s, width), lambda b, p: (b, 0, blk(p)))
    cx = lambda width, blk: pl.BlockSpec((1, lc, width), lambda b, p: (b, 0, blk(p)))
    q_blk = lambda p: p
    k_blk = lambda p: pairs + p
    v_blk = lambda p: 2 * pairs * LANES // dv2 + p
    r_blk = lambda p: (2 * pairs * LANES + GLA_HEADS * GLA_DV) // dv2 + p
    gd_blk = lambda p: gd_block
    return pl.pallas_call(
        _gla_kernel,
        grid=(bsz, pairs),
        in_specs=[lat(LANES, q_blk), lat(LANES, k_blk), lat(dv2, v_blk), lat(dv2, r_blk), lat(LANES, gd_blk),
                  cx(LANES, k_blk), cx(dv2, v_blk), cx(LANES, gd_blk),
                  pl.BlockSpec((2, 1, LANES, LANES), lambda b, p: (0, p, 0, 0)),
                  pl.BlockSpec((2, 1, 1, LANES), lambda b, p: (0, p, 0, 0)),
                  pl.BlockSpec((1, GLA_DV), lambda b, p: (0, 0))],
        out_specs=pl.BlockSpec((1, s, dv2), lambda b, p: (b, 0, p)),
        out_shape=jax.ShapeDtypeStruct((bsz, s, GLA_HEADS * GLA_DV), F32),
        scratch_shapes=[pltpu.VMEM((2, s, LANES), F32)] * 4 + [pltpu.VMEM((2, lc, LANES), F32)] * 2
                       + [pltpu.VMEM((dv2, LANES), F32), pltpu.VMEM((s, dv2), F32)],
        compiler_params=_cparams("parallel", "arbitrary"),
        name="gla",
    )(u, u, u, u, u, uc, uc, uc, g2p, gbp, norm_g)


def _diff_kernel(q_ref, k_ref, v_ref, kc_ref, vc_ref, lam_ref, ng_ref, y_ref, k_s, v_s, *, lambda_init):
    scale = DIFF_DH ** -0.5
    s_len = k_ref.shape[1]

    @pl.when(pl.program_id(2) == 0)
    def _():
        k_s[0:s_len, :] = k_ref[0].astype(BF16)
        v_s[0:s_len, :] = v_ref[0].astype(BF16)
        k_s[s_len:, :] = kc_ref[0].astype(BF16)
        v_s[s_len:, :] = vc_ref[0].astype(BF16)

    lp = lam_ref[...]
    lam = (jnp.exp(jnp.sum(lp[0:1] * lp[1:2], axis=-1, keepdims=True))
           - jnp.exp(jnp.sum(lp[2:3] * lp[3:4], axis=-1, keepdims=True)) + lambda_init)
    low = _iota((1, LANES), 1) < DIFF_DH
    tq = q_ref.shape[1]
    part = tq // DIFF_Q_PARTS
    scores = []
    for i in range(DIFF_Q_PARTS):
        qs = q_ref[0, i * part:(i + 1) * part, :] * scale
        scores += [_mm_nt(jnp.where(low, qs, 0.0), k_s[...]), _mm_nt(jnp.where(low, 0.0, qs), k_s[...])]
    probs = [jnp.exp(s - jnp.max(s, axis=-1, keepdims=True)) for s in scores]
    outs = [_mm(p, v_s[...]) / jnp.sum(p, axis=-1, keepdims=True) for p in probs]
    for i in range(DIFF_Q_PARTS):
        o = outs[2 * i] - lam * outs[2 * i + 1]
        on = o * lax.rsqrt(jnp.mean(o * o, axis=-1, keepdims=True) + RMS_EPS) * ng_ref[...]
        y_ref[0, i * part:(i + 1) * part, :] = on * (1.0 - lambda_init)


def _diff(u, uc, lam_params, norm_g, col0, lambda_init, *, tq):
    bsz, s, _ = u.shape
    lc = uc.shape[1]
    qb = col0 // LANES
    kb, vb = qb + DIFF_HEADS, qb + 2 * DIFF_HEADS
    full = lambda arr_len, cb: pl.BlockSpec((1, arr_len, LANES), lambda b, h, j: (b, 0, cb + h))
    return pl.pallas_call(
        functools.partial(_diff_kernel, lambda_init=lambda_init),
        grid=(bsz, DIFF_HEADS, s // tq),
        in_specs=[pl.BlockSpec((1, tq, LANES), lambda b, h, j: (b, j, qb + h)),
                  full(s, kb), full(s, vb), full(lc, kb), full(lc, vb),
                  pl.BlockSpec((4, DIFF_DH), lambda b, h, j: (0, 0)),
                  pl.BlockSpec((1, DIFF_DV), lambda b, h, j: (0, 0))],
        out_specs=pl.BlockSpec((1, tq, LANES), lambda b, h, j: (b, j, h)),
        out_shape=jax.ShapeDtypeStruct((bsz, s, DIFF_HEADS * DIFF_DV), F32),
        scratch_shapes=[pltpu.VMEM((s + lc, LANES), BF16), pltpu.VMEM((s + lc, LANES), BF16)],
        compiler_params=_cparams("parallel", "parallel", "arbitrary"),
        name="diff_attn",
    )(u, u, u, uc, uc, lam_params, norm_g)


def _head_selector(width, head_dim, value):
    idx = jnp.arange(width) // head_dim
    return jnp.where(idx[:, None] == idx[None, :], value, 0.0).astype(F32)


def _pad_rows(w, lo, total):
    return jnp.zeros((total, w.shape[1]), w.dtype).at[lo:lo + w.shape[0]].set(w)


def _rope_tables(s, reps):
    pos = jnp.arange(s)
    n = DIFF_DH // 4
    freqs = ROPE_BASE ** (-jnp.arange(n, dtype=F32) / n)
    ang_r = (pos // GRID_W).astype(F32)[:, None] * freqs[None, :]
    ang_c = (pos % GRID_W).astype(F32)[:, None] * freqs[None, :]
    cos = jnp.concatenate([jnp.cos(ang_r)] * 2 + [jnp.cos(ang_c)] * 2, axis=-1)
    sin = jnp.concatenate([-jnp.sin(ang_r), jnp.sin(ang_r), -jnp.sin(ang_c), jnp.sin(ang_c)], axis=-1)
    return jnp.tile(cos, (1, reps)), jnp.tile(sin, (1, reps))


def _rwkv_na_mixer(xl, xc, mods, ctx_row, w_in, mu, w0, w2, a0, a2, g2, k_k, k_a, r_k, gn_g, gn_b, rpb):
    bsz, s, _ = xl.shape
    lc = xc.shape[1]
    n = w_in.shape[1]
    chunks = tuple((lo, min(lo + 512, n)) for lo in range(0, n, 512))
    u = _proj(xl, mods, w_in, chunks, tm=512)
    uc = _proj(xc, mods, w_in, chunks, tm=lc, ctx_row=ctx_row)

    eh = _head_selector(RWKV_WIDTH, RWKV_HEAD_DIM, 1.0).astype(BF16)
    em = _head_selector(RWKV_WIDTH, RWKV_HEAD_DIM, 1.0 / RWKV_HEAD_DIM).astype(BF16)
    p = dict(mu=mu[None], w0=w0, a0=a0, g2=g2.astype(BF16), k_k=k_k[None], k_a=k_a[None], r_k=r_k.reshape(1, -1), eh=eh,
             w2=jnp.stack([_pad_rows(w2[0], 0, 128), _pad_rows(w2[1], 64, 128)]).astype(BF16),
             a2=jnp.stack([_pad_rows(a2[0], 0, 128), _pad_rows(a2[1], 64, 128)]).astype(BF16))
    lanes = bsz * RWKV_HEADS
    state = jnp.zeros((2, RWKV_HEAD_DIM * RWKV_HEAD_DIM, lanes), F32)
    parts = []
    for uu, tt in ((uc, lc), (u, 256)):
        *scan_in, bonus, gate = _rwkv_prep(uu, p, tt=tt)
        y2, state = _rwkv_scan(*[_to_time_major(z) for z in scan_in], state)
        parts.append((_from_time_major(y2, bsz), bonus, gate, gn_g[None], gn_b[None], em))
    yc_a, y_a = parts
    y_b, yc_b = _na(u, uc, _na_bias_table(rpb, s // GRID_W), RWKV_IN)
    return (y_a, y_b), (yc_a, yc_b)


def _gla_diff_mixer(xl, xc, mods, ctx_row, w_in, gla_g2, gla_gb, gla_norm_g, diff_lambda, diff_norm_g, lambda_init):
    bsz, s, d = xl.shape
    lc = xc.shape[1]
    n_gla = 2 * GLA_HEADS * GLA_DK + 2 * GLA_HEADS * GLA_DV
    n_gate = 2 * GLA_GATE_LORA
    n_diff = 2 * DIFF_HEADS * 2 * DIFF_DH + DIFF_HEADS * DIFF_DV
    w_perm = jnp.concatenate([w_in[:, :n_gla], w_in[:, n_gla + n_gate:n_gla + n_gate + n_diff],
                              w_in[:, n_gla:n_gla + n_gate], jnp.zeros((d, LANES - n_gate), w_in.dtype)], axis=1).astype(BF16)
    rope_w = 2 * DIFF_HEADS * 2 * DIFF_DH
    chunks = ((0, 512), (512, 1024), (1024, n_gla), (n_gla, n_gla + rope_w),
              (n_gla + rope_w, n_gla + n_diff), (n_gla + n_diff, n_gla + n_diff + LANES))
    cos_t, sin_t = _rope_tables(s, rope_w // DIFF_DH)
    u = _proj(xl, mods, w_perm, chunks, tm=512, rope=(3, cos_t, sin_t))
    uc = _proj(xc, mods, w_perm, chunks, tm=lc, ctx_row=ctx_row)

    pairs = GLA_HEADS // 2
    g2p = jnp.stack([jnp.stack([_pad_rows(gla_g2[dd][:, p * LANES:(p + 1) * LANES], dd * GLA_GATE_LORA, LANES)
                                for p in range(pairs)]) for dd in range(2)]).astype(BF16)
    gbp = gla_gb.reshape(2, pairs, 1, LANES)
    y_c = _gla(u, uc, g2p, gbp, gla_norm_g[None], (n_gla + n_diff) // LANES)
    y_d = _diff(u, uc, diff_lambda, diff_norm_g[None], n_gla, lambda_init, tq=1024)
    return y_c, y_d


def kernel(x, c, ctx, c_ctx, w_mod_0, b_mod_0, ln_g_0, ln_b_0, ffn_gu_0, ffn_down_0, w_in_0, w_out_0, rwkv_mu_0, rwkv_w0_0, rwkv_w2_0, rwkv_a0_0, rwkv_a2_0, rwkv_g2_0, rwkv_k_k_0, rwkv_k_a_0, rwkv_r_k_0, rwkv_gn_g_0, rwkv_gn_b_0, na_rpb_0, w_mod_1, b_mod_1, ln_g_1, ln_b_1, ffn_gu_1, ffn_down_1, w_in_1, w_out_1, gla_g2_1, gla_gb_1, gla_norm_g_1, diff_lambda_1, diff_norm_g_1):
    bsz, s, d = x.shape
    lc = ctx.shape[1]
    ctx_row = bsz
    rows = -(-(bsz + 1) // SUBLANES) * SUBLANES
    c_all = jnp.concatenate([c, c_ctx[None], jnp.zeros((rows - bsz - 1, d), F32)], axis=0)
    tm = 512
    tf = 512

    mods = _mods(c_all, w_mod_0, b_mod_0)
    gu, dn = ffn_gu_0.astype(BF16), ffn_down_0.astype(BF16)
    xl = _ffn(x, mods, 0, gu[0], dn[0], ln_g_0[0], ln_b_0[0], tm=tf)
    xc = _ffn(ctx, mods, 0, gu[0], dn[0], ln_g_0[0], ln_b_0[0], tm=lc, ctx_row=ctx_row)
    (y_a, y_b), (yc_a, yc_b) = _rwkv_na_mixer(xl, xc, mods, ctx_row, w_in_0.astype(BF16), rwkv_mu_0, rwkv_w0_0, rwkv_w2_0,
                                              rwkv_a0_0, rwkv_a2_0, rwkv_g2_0, rwkv_k_k_0, rwkv_k_a_0, rwkv_r_k_0,
                                              rwkv_gn_g_0, rwkv_gn_b_0, na_rpb_0)
    wo = w_out_0.astype(BF16)
    xl = _rwkv_mixout(xl, mods, *y_a, y_b, wo, ln_g_0[1], ln_b_0[1], tm=tm)
    xc = _rwkv_mixout(xc, mods, *yc_a, yc_b, wo, ln_g_0[1], ln_b_0[1], tm=lc, ctx_row=ctx_row)
    xl = _ffn(xl, mods, 6, gu[1], dn[1], ln_g_0[2], ln_b_0[2], tm=tf)
    xc = _ffn(xc, mods, 6, gu[1], dn[1], ln_g_0[2], ln_b_0[2], tm=lc, ctx_row=ctx_row)

    mods = _mods(c_all, w_mod_1, b_mod_1)
    gu, dn = ffn_gu_1.astype(BF16), ffn_down_1.astype(BF16)
    xl = _ffn(xl, mods, 0, gu[0], dn[0], ln_g_1[0], ln_b_1[0], tm=tf)
    xc = _ffn(xc, mods, 0, gu[0], dn[0], ln_g_1[0], ln_b_1[0], tm=lc, ctx_row=ctx_row)
    y_c, y_d = _gla_diff_mixer(xl, xc, mods, ctx_row, w_in_1, gla_g2_1, gla_gb_1, gla_norm_g_1, diff_lambda_1, diff_norm_g_1,
                               0.8 - 0.6 * math.exp(-0.3 * 1))
    xl = _mixout(xl, mods, y_c, y_d, w_out_1.astype(BF16), ln_g_1[1], ln_b_1[1], tm=tm)
    xl = _ffn(xl, mods, 6, gu[1], dn[1], ln_g_1[2], ln_b_1[2], tm=tf)
    return xl
```

```python
import functools
import math

import jax
import jax.numpy as jnp
from jax import lax
from jax.experimental import pallas as pl
from jax.experimental.pallas import tpu as pltpu

F32 = jnp.float32
BF16 = jnp.bfloat16

GRID_W = 64
N_MOD = 9
DEPTH = 2
LN_EPS = 1e-5
RMS_EPS = 1e-6
DEEPNORM_ALPHA = (2 * DEPTH) ** 0.25
RWKV_HEADS = 8
RWKV_HEAD_DIM = 64
RWKV_WIDTH = RWKV_HEADS * RWKV_HEAD_DIM
RWKV_IN = 3 * RWKV_WIDTH + 4 * 64 + 128
RWKV_GN_EPS = 64e-5
RWKV_BLOCK = 32
NA_HEADS = 8
NA_HEAD_DIM = 64
NA_WIN_R = 8
NA_WIN_C = 16
NA_ROW_GROUP = 8
GLA_HEADS = 4
GLA_DK = 64
GLA_DV = 128
GLA_GATE_LORA = 16
GLA_GATE_NORM = 16.0
GLA_SUB = 16
GLA_GROUP = 8
DIFF_HEADS = 4
DIFF_DH = 64
DIFF_DV = 128
DIFF_Q_PARTS = 4
ROPE_BASE = 10000.0

LANES = 128
SUBLANES = 8
V7X_VMEM_BYTES = 64 * 1024 * 1024
VMEM_LIMIT = (V7X_VMEM_BYTES * 7) // 8


def _cparams(*sem):
    return pltpu.CompilerParams(dimension_semantics=sem, vmem_limit_bytes=VMEM_LIMIT)


def _mm(a, b):
    return jnp.dot(a.astype(BF16), b.astype(BF16), preferred_element_type=F32)


def _mm_nt(a, b):
    return lax.dot_general(a.astype(BF16), b.astype(BF16), (((1,), (1,)), ((), ())), preferred_element_type=F32)


def _mm_tn(a, b):
    return lax.dot_general(a.astype(BF16), b.astype(BF16), (((0,), (0,)), ((), ())), preferred_element_type=F32)


def _split3(x):
    hi = x.astype(BF16)
    r1 = x - hi.astype(F32)
    mid = r1.astype(BF16)
    lo = (r1 - mid.astype(F32)).astype(BF16)
    return hi, mid, lo


def _sel_right(x, sel):
    m = x.shape[0]
    y = jnp.dot(jnp.concatenate(_split3(x), axis=0), sel, preferred_element_type=F32)
    return y[0:m] + y[m:2 * m] + y[2 * m:3 * m]


def _sel_left(sel, x):
    n = x.shape[1]
    y = jnp.dot(sel, jnp.concatenate(_split3(x), axis=1), preferred_element_type=F32)
    return y[:, 0:n] + y[:, n:2 * n] + y[:, 2 * n:3 * n]


def _sigmoid(x):
    return 1.0 / (1.0 + jnp.exp(-x))


def _log_sigmoid(x):
    return -(jnp.maximum(-x, 0.0) + jnp.log(1.0 + jnp.exp(-jnp.abs(x))))


def _layer_norm(z, g, b):
    mu = jnp.mean(z, axis=-1, keepdims=True)
    d = z - mu
    var = jnp.mean(d * d, axis=-1, keepdims=True)
    return d * lax.rsqrt(var + LN_EPS) * g + b


def _iota(shape, dim):
    return lax.broadcasted_iota(jnp.int32, shape, dim)


def _mods_kernel(c_ref, w_ref, b_ref, o_ref):
    c = c_ref[...]
    o_ref[...] = _mm(c * _sigmoid(c), w_ref[...]) + b_ref[...]


def _mods(c_all, w_mod, b_mod):
    rows, d = c_all.shape
    n = w_mod.shape[1]
    tn = n // 8
    out = pl.pallas_call(
        _mods_kernel,
        grid=(n // tn,),
        in_specs=[pl.BlockSpec((rows, d), lambda j: (0, 0)),
                  pl.BlockSpec((d, tn), lambda j: (0, j)),
                  pl.BlockSpec((1, tn), lambda j: (0, j))],
        out_specs=pl.BlockSpec((rows, tn), lambda j: (0, j)),
        out_shape=jax.ShapeDtypeStruct((rows, n), F32),
        compiler_params=_cparams("arbitrary"),
        name="mods",
    )(c_all, w_mod, b_mod[None])
    return out.reshape(rows, N_MOD, d)


def _mod_spec(d, ctx_row):
    if ctx_row is None:
        return pl.BlockSpec((1, N_MOD, d), lambda b, j: (b, 0, 0))
    return pl.BlockSpec((1, N_MOD, d), lambda b, j: (ctx_row, 0, 0))


def _ffn_kernel(x_ref, m_ref, wgu_ref, wd_ref, g_ref, b_ref, o_ref, *, i0, fc):
    hidden = wd_ref.shape[0]
    x = x_ref[0]
    shift, scale, gate = (m_ref[0, i0 + i:i0 + i + 1, :] for i in range(3))
    h = (x * (1.0 + scale) + shift).astype(BF16)
    acc = None
    for c in range(hidden // fc):
        g = jnp.dot(h, wgu_ref[:, c * fc:(c + 1) * fc], preferred_element_type=F32)
        u = jnp.dot(h, wgu_ref[:, hidden + c * fc:hidden + (c + 1) * fc], preferred_element_type=F32)
        a = (g * _sigmoid(g) * u).astype(BF16)
        dn = jnp.dot(a, wd_ref[c * fc:(c + 1) * fc, :], preferred_element_type=F32)
        acc = dn if acc is None else acc + dn
    z = DEEPNORM_ALPHA * x + gate * (0.5 * acc)
    o_ref[0] = _layer_norm(z, g_ref[...], b_ref[...])


def _ffn(x, mods, i0, w_gu, w_down, ln_g, ln_b, *, tm, ctx_row=None):
    bsz, t, d = x.shape
    hidden = w_down.shape[0]
    const = lambda b, j: (0, 0)
    return pl.pallas_call(
        functools.partial(_ffn_kernel, i0=i0, fc=256),
        grid=(bsz, t // tm),
        in_specs=[pl.BlockSpec((1, tm, d), lambda b, j: (b, j, 0)),
                  _mod_spec(d, ctx_row),
                  pl.BlockSpec((d, 2 * hidden), const, pipeline_mode=pl.Buffered(1)),
                  pl.BlockSpec((hidden, d), const, pipeline_mode=pl.Buffered(1)),
                  pl.BlockSpec((1, d), const),
                  pl.BlockSpec((1, d), const)],
        out_specs=pl.BlockSpec((1, tm, d), lambda b, j: (b, j, 0)),
        out_shape=jax.ShapeDtypeStruct(x.shape, F32),
        compiler_params=_cparams("parallel", "parallel"),
        name="ffn",
    )(x, mods, w_gu, w_down, ln_g[None], ln_b[None])


def _proj_kernel(x_ref, m_ref, w_ref, *rest, chunks, rope_chunk):
    if rope_chunk is None:
        (o_ref,) = rest
    else:
        cos_ref, sin_ref, o_ref = rest
    x = x_ref[0]
    h = (x * (1.0 + m_ref[0, 4:5, :]) + m_ref[0, 3:4, :]).astype(BF16)
    for ci, (lo, hi) in enumerate(chunks):
        u = jnp.dot(h, w_ref[:, lo:hi], preferred_element_type=F32)
        if ci == rope_chunk:
            n = hi - lo
            first = (_iota((1, n), 1) & 16) == 0
            partner = jnp.where(first, pltpu.roll(u, n - 16, axis=1), pltpu.roll(u, 16, axis=1))
            u = u * cos_ref[...] + partner * sin_ref[...]
        o_ref[0, :, lo:hi] = u


def _proj(x, mods, w_in, chunks, *, tm, ctx_row=None, rope=None):
    bsz, t, d = x.shape
    n = w_in.shape[1]
    const = lambda b, j: (0, 0)
    in_specs = [pl.BlockSpec((1, tm, d), lambda b, j: (b, j, 0)),
                _mod_spec(d, ctx_row),
                pl.BlockSpec((d, n), const, pipeline_mode=pl.Buffered(1))]
    args = [x, mods, w_in]
    rope_chunk = None
    if rope is not None:
        rope_chunk, cos_t, sin_t = rope
        width = cos_t.shape[1]
        in_specs += [pl.BlockSpec((tm, width), lambda b, j: (j, 0))] * 2
        args += [cos_t, sin_t]
    return pl.pallas_call(
        functools.partial(_proj_kernel, chunks=chunks, rope_chunk=rope_chunk),
        grid=(bsz, t // tm),
        in_specs=in_specs,
        out_specs=pl.BlockSpec((1, tm, n), lambda b, j: (b, j, 0)),
        out_shape=jax.ShapeDtypeStruct((bsz, t, n), F32),
        compiler_params=_cparams("parallel", "parallel"),
        name="proj",
    )(*args)


def _mixout_kernel(x_ref, m_ref, ya_ref, yb_ref, w_ref, g_ref, b_ref, o_ref):
    x = x_ref[0]
    half = ya_ref.shape[2]
    y = _mm(ya_ref[0], w_ref[0:half, :]) + _mm(yb_ref[0], w_ref[half:2 * half, :])
    z = DEEPNORM_ALPHA * x + m_ref[0, 5:6, :] * y
    o_ref[0] = _layer_norm(z, g_ref[...], b_ref[...])


def _rwkv_mixout_kernel(x_ref, m_ref, yf_ref, yr_ref, bonus_ref, gate_ref, gng_ref, gnb_ref, em_ref, yb_ref, w_ref,
                        g_ref, b_ref, o_ref):
    x = x_ref[0]
    half = yb_ref.shape[2]
    y = yf_ref[0, 0] + yr_ref[0, 0]
    em = em_ref[...]
    dlt = y - _sel_right(y, em)
    var = _sel_right(dlt * dlt, em)
    ya = (dlt * lax.rsqrt(var + RWKV_GN_EPS) * gng_ref[...] + gnb_ref[...] + bonus_ref[0]) * gate_ref[0]
    y = _mm(ya, w_ref[0:half, :]) + _mm(yb_ref[0], w_ref[half:2 * half, :])
    z = DEEPNORM_ALPHA * x + m_ref[0, 5:6, :] * y
    o_ref[0] = _layer_norm(z, g_ref[...], b_ref[...])


def _rwkv_mixout(x, mods, y2, bonus, gate, gn_g, gn_b, em, yb, w_out, ln_g, ln_b, *, tm, ctx_row=None):
    bsz, t, d = x.shape
    half = yb.shape[2]
    const = lambda b, j: (0, 0)
    tok = lambda b, j: (b, j, 0)
    head = pl.BlockSpec((1, tm, half), tok)
    return pl.pallas_call(
        _rwkv_mixout_kernel,
        grid=(bsz, t // tm),
        in_specs=[pl.BlockSpec((1, tm, d), tok), _mod_spec(d, ctx_row),
                  pl.BlockSpec((1, 1, tm, half), lambda b, j: (0, b, j, 0)),
                  pl.BlockSpec((1, 1, tm, half), lambda b, j: (1, b, j, 0)),
                  head, head, pl.BlockSpec((1, half), const), pl.BlockSpec((1, half), const),
                  pl.BlockSpec((half, half), const), head,
                  pl.BlockSpec((2 * half, d), const, pipeline_mode=pl.Buffered(1)),
                  pl.BlockSpec((1, d), const), pl.BlockSpec((1, d), const)],
        out_specs=pl.BlockSpec((1, tm, d), tok),
        out_shape=jax.ShapeDtypeStruct(x.shape, F32),
        compiler_params=_cparams("parallel", "parallel"),
        name="rwkv_mixout",
    )(x, mods, y2, y2, bonus, gate, gn_g, gn_b, em, yb, w_out, ln_g[None], ln_b[None])


def _mixout(x, mods, ya, yb, w_out, ln_g, ln_b, *, tm, ctx_row=None):
    bsz, t, d = x.shape
    half = ya.shape[2]
    const = lambda b, j: (0, 0)
    tok = lambda b, j: (b, j, 0)
    return pl.pallas_call(
        _mixout_kernel,
        grid=(bsz, t // tm),
        in_specs=[pl.BlockSpec((1, tm, d), tok), _mod_spec(d, ctx_row),
                  pl.BlockSpec((1, tm, half), tok), pl.BlockSpec((1, tm, half), tok),
                  pl.BlockSpec((2 * half, d), const, pipeline_mode=pl.Buffered(1)),
                  pl.BlockSpec((1, d), const), pl.BlockSpec((1, d), const)],
        out_specs=pl.BlockSpec((1, tm, d), tok),
        out_shape=jax.ShapeDtypeStruct(x.shape, F32),
        compiler_params=_cparams("parallel", "parallel"),
        name="mixout",
    )(x, mods, ya, yb, w_out, ln_g[None], ln_b[None])


def _rwkv_prep_kernel(u_ref, up_ref, un_ref, mu_ref, w0_ref, w2_ref, a0_ref, a2_ref, g2_ref, kk_ref, ka_ref, rk_ref,
                      eh_ref, cum_ref, v_o, a1_o, a2_o, b1_o, b2_o, wend_o, bonus_o, g_o, wfull_s, *, tt):
    j = pl.program_id(1)
    nj = pl.num_programs(1)
    wdt = RWKV_WIDTH
    u = u_ref[0]
    prev_row = up_ref[0, SUBLANES - 1:SUBLANES, :] * (j > 0).astype(F32)
    next_row = un_ref[0, 0:1, :] * (j < nj - 1).astype(F32)
    row = _iota((tt, 1), 0)
    up = jnp.where(row == 0, prev_row, pltpu.roll(u, 1, axis=0))
    un = jnp.where(row == tt - 1, next_row, pltpu.roll(u, tt - 1, axis=0))
    us = u + mu_ref[...] * (0.5 * (up + un) - u)
    r, k, v = us[:, 0:wdt], us[:, wdt:2 * wdt], us[:, 2 * wdt:3 * wdt]
    wd = jnp.tanh(us[:, 3 * wdt:3 * wdt + 128])
    ad = us[:, 3 * wdt + 128:3 * wdt + 256]
    gd = us[:, 3 * wdt + 256:3 * wdt + 384]
    eh = eh_ref[...]
    kkn = k * kk_ref[...]
    kkn = kkn / jnp.maximum(jnp.sqrt(_sel_right(kkn * kkn, eh)), 1e-12)
    bonus_o[0] = _sel_right(r * k * rk_ref[...], eh) * v
    g_o[0] = _mm(_sigmoid(gd), g2_ref[...])
    v_o[0] = v
    blk = RWKV_BLOCK
    for d in range(2):
        wl = w0_ref[d:d + 1, :] + _mm(wd, w2_ref[d])
        e = jnp.exp(_log_sigmoid(wl) - 0.5)
        e_inc = _sel_left(cum_ref[d], e)
        w_inc = jnp.exp(-e_inc)
        inv_w = jnp.exp(e_inc)
        a = _sigmoid(a0_ref[d:d + 1, :] + _mm(ad, a2_ref[d]))
        a1_o[d, 0] = kkn * jnp.exp(e - e_inc)
        a2_o[d, 0] = r * w_inc
        b1_o[d, 0] = kkn * a * inv_w
        b2_o[d, 0] = k * (1.0 + (a - 1.0) * ka_ref[...]) * inv_w
        last = blk - 1 if d == 0 else 0
        for c in range(wdt // LANES):
            cols = slice(c * LANES, (c + 1) * LANES)
            wfull_s[c] = w_inc[:, cols]
            wend_o[d, 0, :, cols] = wfull_s[c, pl.ds(last, tt // blk, stride=blk), :]


def _rwkv_prep(u, p, *, tt):
    bsz, t, _ = u.shape
    wdt = RWKV_WIDTH
    nblk8 = t // SUBLANES
    per = tt // SUBLANES
    c2 = lambda b, j: (0, 0)
    c3 = lambda b, j: (0, 0, 0)
    tok = lambda b, j: (b, j, 0)
    tok2 = lambda b, j: (0, b, j, 0)
    one = jax.ShapeDtypeStruct((bsz, t, wdt), F32)
    two = jax.ShapeDtypeStruct((2, bsz, t, wdt), F32)
    blk = RWKV_BLOCK
    ends = jax.ShapeDtypeStruct((2, bsz, t // blk, wdt), F32)
    ri = jnp.arange(tt)[:, None]
    ci = jnp.arange(tt)[None, :]
    same = (ri // blk) == (ci // blk)
    cum_sel = jnp.stack([same & (ci <= ri), same & (ci >= ri)]).astype(BF16)
    return pl.pallas_call(
        functools.partial(_rwkv_prep_kernel, tt=tt),
        grid=(bsz, t // tt),
        in_specs=[pl.BlockSpec((1, tt, RWKV_IN), tok),
                  pl.BlockSpec((1, SUBLANES, RWKV_IN), lambda b, j: (b, jnp.maximum(j * per - 1, 0), 0)),
                  pl.BlockSpec((1, SUBLANES, RWKV_IN), lambda b, j: (b, jnp.minimum((j + 1) * per, nblk8 - 1), 0)),
                  pl.BlockSpec((1, RWKV_IN), c2),
                  pl.BlockSpec((2, wdt), c2), pl.BlockSpec((2, 128, wdt), c3),
                  pl.BlockSpec((2, wdt), c2), pl.BlockSpec((2, 128, wdt), c3),
                  pl.BlockSpec((128, wdt), c2),
                  pl.BlockSpec((1, wdt), c2), pl.BlockSpec((1, wdt), c2), pl.BlockSpec((1, wdt), c2),
                  pl.BlockSpec((wdt, wdt), c2), pl.BlockSpec((2, tt, tt), c3)],
        out_specs=[pl.BlockSpec((1, tt, wdt), tok)] + [pl.BlockSpec((2, 1, tt, wdt), tok2)] * 4
                  + [pl.BlockSpec((2, 1, tt // blk, wdt), tok2)] + [pl.BlockSpec((1, tt, wdt), tok)] * 2,
        out_shape=[one, two, two, two, two, ends, one, one],
        scratch_shapes=[pltpu.VMEM((wdt // LANES, tt, LANES), F32)],
        compiler_params=_cparams("parallel", "parallel"),
        name="rwkv_prep",
    )(u, u, u, p["mu"], p["w0"], p["w2"], p["a0"], p["a2"], p["g2"], p["k_k"], p["k_a"], p["r_k"], p["eh"], cum_sel)


def _rwkv_scan_kernel(v_ref, a1_ref, a2_ref, b1_ref, b2_ref, wend_ref, s0_ref, y_ref, sT_ref, s_ref, *, k_unroll):
    d = pl.program_id(0)
    j = pl.program_id(1)
    tb = RWKV_BLOCK
    nk = RWKV_HEAD_DIM
    nvb = RWKV_HEAD_DIM // SUBLANES
    lanes = s_ref.shape[1]

    @pl.when(j == 0)
    def _():
        s_ref[...] = s0_ref[...]

    def bcast(ref, t, k):
        return jnp.broadcast_to(ref[t, pl.ds(k, 1), :], (SUBLANES, lanes))

    def srow(k, vb):
        return pl.ds(pl.multiple_of(k * nk + vb * SUBLANES, SUBLANES), SUBLANES)

    def step(i, carry):
        t = jnp.where(d == 0, i, tb - 1 - i)

        def reduce_body(kc, acc):
            acc = list(acc)
            for kq in range(k_unroll):
                k = kc * k_unroll + kq
                a1b = bcast(a1_ref, t, k)
                a2b = bcast(a2_ref, t, k)
                for vb in range(nvb):
                    s = s_ref[srow(k, vb), :]
                    acc[vb] = acc[vb] + s * a1b
                    acc[nvb + vb] = acc[nvb + vb] + s * a2b
            return tuple(acc)

        zero = jnp.zeros((SUBLANES, lanes), F32)
        acc = lax.fori_loop(0, nk // k_unroll, reduce_body, (zero,) * (2 * nvb))
        skk = acc[:nvb]
        a2t = a2_ref[t]
        c1 = jnp.sum(b1_ref[t] * a2t, axis=0, keepdims=True)
        c2 = jnp.sum(b2_ref[t] * a2t, axis=0, keepdims=True)
        vt = [v_ref[t, vb * SUBLANES:(vb + 1) * SUBLANES, :] for vb in range(nvb)]
        for vb in range(nvb):
            y_ref[t, vb * SUBLANES:(vb + 1) * SUBLANES, :] = acc[nvb + vb] - skk[vb] * c1 + vt[vb] * c2

        def update_body(kc, c):
            for kq in range(k_unroll):
                k = kc * k_unroll + kq
                b1b = bcast(b1_ref, t, k)
                b2b = bcast(b2_ref, t, k)
                for vb in range(nvb):
                    idx = srow(k, vb)
                    s_ref[idx, :] = s_ref[idx, :] - skk[vb] * b1b + vt[vb] * b2b
            return c

        lax.fori_loop(0, nk // k_unroll, update_body, 0)
        return carry

    lax.fori_loop(0, tb, step, 0)

    def renorm_body(kc, c):
        for kq in range(k_unroll):
            k = kc * k_unroll + kq
            wb = bcast(wend_ref, 0, k)
            for vb in range(nvb):
                idx = srow(k, vb)
                s_ref[idx, :] = s_ref[idx, :] * wb
        return c

    lax.fori_loop(0, nk // k_unroll, renorm_body, 0)

    @pl.when(j == pl.num_programs(1) - 1)
    def _():
        sT_ref[...] = s_ref[...]


def _rwkv_scan(v, a1, a2, b1, b2, wend, s0):
    t, nk, lanes = v.shape
    tb = RWKV_BLOCK
    nblk = t // tb
    blk_idx = lambda d, j: j + d * (nblk - 1 - 2 * j)
    shared = pl.BlockSpec((tb, nk, lanes), lambda d, j: (blk_idx(d, j), 0, 0))
    perdir = pl.BlockSpec((None, tb, nk, lanes), lambda d, j: (d, blk_idx(d, j), 0, 0))
    ends = pl.BlockSpec((None, 1, nk, lanes), lambda d, j: (d, blk_idx(d, j), 0, 0))
    state = pl.BlockSpec((None, nk * nk, lanes), lambda d, j: (d, 0, 0))
    return pl.pallas_call(
        functools.partial(_rwkv_scan_kernel, k_unroll=16),
        grid=(2, nblk),
        in_specs=[shared, perdir, perdir, perdir, perdir, ends, state],
        out_specs=[perdir, state],
        out_shape=[jax.ShapeDtypeStruct((2, t, nk, lanes), F32), jax.ShapeDtypeStruct((2, nk * nk, lanes), F32)],
        scratch_shapes=[pltpu.VMEM((nk * nk, lanes), F32)],
        compiler_params=_cparams("arbitrary", "arbitrary"),
        name="rwkv_scan",
    )(v, a1, a2, b1, b2, wend, s0)


def _to_time_major(a):
    *lead, bsz, t, _ = a.shape
    nl = len(lead)
    a = a.reshape(*lead, bsz, t, RWKV_HEADS, RWKV_HEAD_DIM)
    a = a.transpose(*range(nl), nl + 1, nl + 3, nl, nl + 2)
    return a.reshape(*lead, t, RWKV_HEAD_DIM, bsz * RWKV_HEADS)


def _from_time_major(a, bsz):
    two, t, n, _ = a.shape
    a = a.reshape(two, t, n, bsz, RWKV_HEADS).transpose(0, 3, 1, 4, 2)
    return a.reshape(two, bsz, t, RWKV_HEADS * n)


def _na_kernel(q_ref, k_ref, v_ref, qc_ref, kc_ref, vc_ref, bias_ref, y_ref, yc_ref, kb_s, vb_s, kcb_s, vcb_s, *, rows):
    scale = NA_HEAD_DIM ** -0.5
    wq = GRID_W
    band = NA_WIN_R * GRID_W
    kb_s[...] = k_ref[0].astype(BF16)
    vb_s[...] = v_ref[0].astype(BF16)
    kcb_s[...] = kc_ref[0].astype(BF16)
    vcb_s[...] = vc_ref[0].astype(BF16)
    low = _iota((1, LANES), 1) < NA_HEAD_DIM

    def stack_heads(qs):
        return jnp.concatenate([jnp.where(low, qs, 0.0), jnp.where(low, 0.0, qs)], axis=0).astype(BF16)

    def softmax_parts(scores):
        m = functools.reduce(jnp.maximum, [jnp.max(s, axis=-1, keepdims=True) for s in scores])
        ps = [jnp.exp(s - m) for s in scores]
        l = functools.reduce(lambda a, b: a + b, [jnp.sum(p, axis=-1, keepdims=True) for p in ps])
        return ps, l

    def combine(ps, l, values):
        o = functools.reduce(lambda a, b: a + b, [_mm(p, v) for p, v in zip(ps, values)]) / l
        m = o.shape[0] // 2
        return jnp.where(low, o[0:m], o[m:2 * m])

    def row_group(i, c):
        jobs = []
        for u in range(NA_ROW_GROUP):
            r = i * NA_ROW_GROUP + u
            rs = jnp.clip(r - NA_WIN_R // 2, 0, rows - NA_WIN_R)
            var = jnp.where(r < NA_WIN_R // 2, r,
                            jnp.where(r > rows - NA_WIN_R // 2, r - (rows - NA_WIN_R), NA_WIN_R // 2))
            q0 = pl.multiple_of(r * wq, wq)
            k0 = pl.multiple_of(rs * wq, wq)
            qst = stack_heads(q_ref[0, pl.ds(q0, wq), :] * scale)
            scores = [_mm_nt(qst, kb_s[pl.ds(k0, band), :]) + bias_ref[0, var], _mm_nt(qst, kcb_s[...])]
            jobs.append((q0, k0, scores))
        soft = [softmax_parts(scores) for _, _, scores in jobs]
        for (q0, k0, _), (ps, l) in zip(jobs, soft):
            y_ref[0, pl.ds(q0, wq), :] = combine(ps, l, [vb_s[pl.ds(k0, band), :], vcb_s[...]])
        return c

    lax.fori_loop(0, rows // NA_ROW_GROUP, row_group, 0)
    ps, l = softmax_parts([_mm_nt(stack_heads(qc_ref[0] * scale), kcb_s[...])])
    yc_ref[0] = combine(ps, l, [vcb_s[...]])


def _na_bias_table(rpb, rows):
    cols = jnp.arange(GRID_W)
    col_start = jnp.clip(cols - NA_WIN_C // 2, 0, GRID_W - NA_WIN_C)
    col_in = (cols[None, :] >= col_start[:, None]) & (cols[None, :] < col_start[:, None] + NA_WIN_C)
    dc_idx = jnp.clip(cols[None, :] - cols[:, None] + NA_WIN_C - 1, 0, 2 * NA_WIN_C - 2)
    rpb_cols = jnp.where(col_in[None, None], rpb[:, :, dc_idx], -jnp.inf)
    half = NA_WIN_R // 2
    rep_rows = list(range(half)) + [half] + list(range(rows - half + 1, rows))
    tiles = []
    for r in rep_rows:
        rs = min(max(r - half, 0), rows - NA_WIN_R)
        dr_idx = rs + jnp.arange(NA_WIN_R) - r + NA_WIN_R - 1
        t = rpb_cols[:, dr_idx]
        tiles.append(t.transpose(0, 2, 1, 3).reshape(NA_HEADS, GRID_W, NA_WIN_R * GRID_W))
    tab = jnp.stack(tiles, axis=1)
    tab = tab.reshape(NA_HEADS // 2, 2, len(rep_rows), GRID_W, NA_WIN_R * GRID_W).transpose(0, 2, 1, 3, 4)
    return tab.reshape(NA_HEADS // 2, len(rep_rows), 2 * GRID_W, NA_WIN_R * GRID_W)


def _na(u, uc, bias_tab, col0):
    bsz, s, _ = u.shape
    lc = uc.shape[1]
    rows = s // GRID_W
    assert rows >= 2 * NA_WIN_R and rows % NA_ROW_GROUP == 0
    qb, kb, vb = col0 // LANES, col0 // LANES + 4, col0 // LANES + 8
    band = NA_WIN_R * GRID_W
    lat = lambda cb: pl.BlockSpec((1, s, LANES), lambda b, p: (b, 0, cb + p))
    cx = lambda cb: pl.BlockSpec((1, lc, LANES), lambda b, p: (b, 0, cb + p))
    return pl.pallas_call(
        functools.partial(_na_kernel, rows=rows),
        grid=(bsz, NA_HEADS // 2),
        in_specs=[lat(qb), lat(kb), lat(vb), cx(qb), cx(kb), cx(vb),
                  pl.BlockSpec((1, NA_WIN_R, 2 * GRID_W, band), lambda b, p: (p, 0, 0, 0))],
        out_specs=[pl.BlockSpec((1, s, LANES), lambda b, p: (b, 0, p)),
                   pl.BlockSpec((1, lc, LANES), lambda b, p: (b, 0, p))],
        out_shape=[jax.ShapeDtypeStruct((bsz, s, NA_HEADS * NA_HEAD_DIM), F32),
                   jax.ShapeDtypeStruct((bsz, lc, NA_HEADS * NA_HEAD_DIM), F32)],
        scratch_shapes=[pltpu.VMEM((s, LANES), BF16), pltpu.VMEM((s, LANES), BF16),
                        pltpu.VMEM((lc, LANES), BF16), pltpu.VMEM((lc, LANES), BF16)],
        compiler_params=_cparams("parallel", "arbitrary"),
        name="na",
    )(u, u, u, uc, uc, uc, bias_tab)


def _gla_kernel(q_ref, k_ref, v_ref, r_ref, gd_ref, kc_ref, vc_ref, gdc_ref, g2_ref, gb_ref, ng_ref, y_ref,
                bc_s, qt_s, kt_s, dec_s, ktc_s, decc_s, st_s, of_s):
    s_len = q_ref.shape[1]
    l_len = kc_ref.shape[1]
    cs = GLA_SUB
    scale = GLA_DK ** -0.5
    ri = _iota((LANES, LANES), 0)
    ci = _iota((LANES, LANES), 1)
    sub_shift = cs.bit_length() - 1
    same_chunk = (ri >> sub_shift) == (ci >> sub_shift)
    lane = _iota((1, LANES), 1)
    head_lane = [lane < GLA_DK, lane >= GLA_DK]
    st_mask = (_iota((2 * GLA_DV, LANES), 0) >= GLA_DV) == (_iota((2 * GLA_DV, LANES), 1) >= GLA_DK)
    trow = _iota((cs, 1), 0)

    cum_sels = [jnp.concatenate([(same_chunk & ((ci <= ri) if d == 0 else (ci >= ri))).astype(F32),
                                 same_chunk.astype(F32)], axis=0).astype(BF16) for d in range(2)]

    def gates(gd):
        gs = [_log_sigmoid(_mm(gd, g2_ref[d, 0]) + gb_ref[d, 0]) / GLA_GATE_NORM for d in range(2)]
        ys = [_sel_left(cum_sels[d], gs[d]) for d in range(2)]
        return [(y[0:LANES], y[LANES:2 * LANES]) for y in ys]

    def prep_lat(i, c):
        rows = pl.ds(pl.multiple_of(i * LANES, LANES), LANES)
        qs = q_ref[0, rows, :] * scale
        kk = k_ref[0, rows, :]
        for d, (bc, tot) in enumerate(gates(gd_ref[0, rows, :])):
            bc_s[d, rows, :] = bc
            qt_s[d, rows, :] = qs * jnp.exp(bc)
            kt_s[d, rows, :] = kk * jnp.exp(tot - bc)
            dec_s[d, rows, :] = jnp.exp(tot)
        return c

    def prep_ctx(i, c):
        rows = pl.ds(pl.multiple_of(i * LANES, LANES), LANES)
        kk = kc_ref[0, rows, :]
        for d, (bc, tot) in enumerate(gates(gdc_ref[0, rows, :])):
            ktc_s[d, rows, :] = kk * jnp.exp(tot - bc)
            decc_s[d, rows, :] = jnp.exp(tot)
        return c

    lax.fori_loop(0, s_len // LANES, prep_lat, 0, unroll=2)
    lax.fori_loop(0, l_len // LANES, prep_ctx, 0, unroll=2)

    for d in range(2):
        fwd = d == 0
        st_s[...] = jnp.zeros(st_s.shape, F32)

        grp = GLA_GROUP
        order = tuple(range(grp)) if fwd else tuple(range(grp - 1, -1, -1))

        def group_starts(i, n_groups, fwd=fwd):
            g = i if fwd else n_groups - 1 - i
            return [(g * grp + u) * cs for u in range(grp)]

        def group_rows(i, n_groups):
            return [pl.ds(pl.multiple_of(r0, cs), cs) for r0 in group_starts(i, n_groups)]

        def increment(kt, vv):
            return jnp.where(st_mask, _mm_tn(vv, kt), 0.0)

        def ctx_group(i, c, order=order):
            rows = group_rows(i, l_len // (cs * grp))
            incs = [increment(ktc_s[d, rows[u], :], vc_ref[0, rows[u], :]) for u in range(grp)]
            st = st_s[...]
            for u in order:
                st = st * decc_s[d, rows[u], :][0:1, :] + incs[u]
            st_s[...] = st
            return c

        def lat_group(i, c, fwd=fwd, order=order):
            starts = group_starts(i, s_len // (cs * grp))
            rows = [pl.ds(pl.multiple_of(r0, cs), cs) for r0 in starts]
            vvs =[v_ref[0, rows[u], :] for u in range(grp)]
            incs = [increment(kt_s[d, rows[u], :], vvs[u]) for u in range(grp)]
            st = st_s[...]
            inter = [None] * grp
            for u in order:
                inter[u] = _mm_nt(qt_s[d, rows[u], :], st)
                st = st * dec_s[d, rows[u], :][0:1, :] + incs[u]
            st_s[...] = st
            half = cs // 2
            rowb = lambda x: jnp.broadcast_to(x, (half, x.shape[-1]))
            trow_h = _iota((half, 1), 0)
            first, second = (0, half) if fwd else (half, 0)
            a_offs = []
            for u in range(grp):
                bc = bc_s[d, rows[u], :]
                qs = q_ref[0, rows[u], :] * scale
                kk = k_ref[0, rows[u], :]
                beta = bc[half - 1:half, :] if fwd else bc[half:half + 1, :]
                q2 = qs[second:second + half] * jnp.exp(bc[second:second + half] - beta)
                k1 = kk[first:first + half] * jnp.exp(beta - bc[first:first + half])
                q2s = jnp.concatenate([jnp.where(head_lane[0], q2, 0.0), jnp.where(head_lane[1], q2, 0.0)], axis=0)
                a_offs.append(_mm_nt(q2s, k1))
            offs = [[_mm(a_offs[u][h * half:(h + 1) * half], vvs[u][first:first + half, h * GLA_DV:(h + 1) * GLA_DV])
                     for h in range(2)] for u in range(grp)]
            for u in range(grp):
                bc = bc_s[d, rows[u], :]
                qs = q_ref[0, rows[u], :] * scale
                off = offs[u]
                halves = {}
                for base in (first, second):
                    bch = bc[base:base + half]
                    qh = qs[base:base + half]
                    o0 = inter[u][base:base + half, 0:GLA_DV]
                    o1 = inter[u][base:base + half, GLA_DV:2 * GLA_DV]
                    if base == second:
                        o0, o1 = o0 + off[0], o1 + off[1]
                    for s in range(half):
                        srow = pl.ds(starts[u] + base + s, 1)
                        keep = (trow_h >= s) if fwd else (trow_h <= s)
                        e = jnp.exp(jnp.where(keep, bch - rowb(bc_s[d, srow, :]), -jnp.inf))
                        term = qh * rowb(k_ref[0, srow, :]) * e
                        a0 = jnp.sum(jnp.where(head_lane[0], term, 0.0), axis=-1, keepdims=True)
                        a1 = jnp.sum(jnp.where(head_lane[1], term, 0.0), axis=-1, keepdims=True)
                        vrow = rowb(v_ref[0, srow, :])
                        o0 = o0 + a0 * vrow[:, 0:GLA_DV]
                        o1 = o1 + a1 * vrow[:, GLA_DV:2 * GLA_DV]
                    halves[base] = (o0, o1)
                o0 = jnp.concatenate([halves[0][0], halves[half][0]], axis=0)
                o1 = jnp.concatenate([halves[0][1], halves[half][1]], axis=0)
                if fwd:
                    of_s[rows[u], 0:GLA_DV] = o0
                    of_s[rows[u], GLA_DV:2 * GLA_DV] = o1
                else:
                    rr = r_ref[0, rows[u], :]
                    for h, oh in enumerate((o0, o1)):
                        cols = slice(h * GLA_DV, (h + 1) * GLA_DV)
                        ot = of_s[rows[u], cols] + oh
                        on = ot * lax.rsqrt(jnp.mean(ot * ot, axis=-1, keepdims=True) + RMS_EPS) * ng_ref[...]
                        rh = rr[:, cols]
                        y_ref[0, rows[u], cols] = on * (rh * _sigmoid(rh))
            return c

        lax.fori_loop(0, l_len // (cs * grp), ctx_group, 0)
        lax.fori_loop(0, s_len // (cs * grp), lat_group, 0)


def _gla(u, uc, g2p, gbp, norm_g, gd_block):
    bsz, s, _ = u.shape
    lc = uc.shape[1]
    pairs = GLA_HEADS // 2
    dv2 = 2 * GLA_DV
    lat = lambda width, blk: pl.BlockSpec((1, s, width), lambda b, p: (b, 0, blk(p)))
    cx = lambda width, blk: pl.BlockSpec((1, lc, width), lambda b, p: (b, 0, blk(p)))
    q_blk = lambda p: p
    k_blk = lambda p: pairs + p
    v_blk = lambda p: 2 * pairs * LANES // dv2 + p
    r_blk = lambda p: (2 * pairs * LANES + GLA_HEADS * GLA_DV) // dv2 + p
    gd_blk = lambda p: gd_block
    return pl.pallas_call(
        _gla_kernel,
        grid=(bsz, pairs),
        in_specs=[lat(LANES, q_blk), lat(LANES, k_blk), lat(dv2, v_blk), lat(dv2, r_blk), lat(LANES, gd_blk),
                  cx(LANES, k_blk), cx(dv2, v_blk), cx(LANES, gd_blk),
                  pl.BlockSpec((2, 1, LANES, LANES), lambda b, p: (0, p, 0, 0)),
                  pl.BlockSpec((2, 1, 1, LANES), lambda b, p: (0, p, 0, 0)),
                  pl.BlockSpec((1, GLA_DV), lambda b, p: (0, 0))],
        out_specs=pl.BlockSpec((1, s, dv2), lambda b, p: (b, 0, p)),
        out_shape=jax.ShapeDtypeStruct((bsz, s, GLA_HEADS * GLA_DV), F32),
        scratch_shapes=[pltpu.VMEM((2, s, LANES), F32)] * 4 + [pltpu.VMEM((2, lc, LANES), F32)] * 2
                       + [pltpu.VMEM((dv2, LANES), F32), pltpu.VMEM((s, dv2), F32)],
        compiler_params=_cparams("parallel", "arbitrary"),
        name="gla",
    )(u, u, u, u, u, uc, uc, uc, g2p, gbp, norm_g)


def _diff_kernel(q_ref, k_ref, v_ref, kc_ref, vc_ref, lam_ref, ng_ref, y_ref, k_s, v_s, *, lambda_init):
    scale = DIFF_DH ** -0.5
    s_len = k_ref.shape[1]

    @pl.when(pl.program_id(2) == 0)
    def _():
        k_s[0:s_len, :] = k_ref[0].astype(BF16)
        v_s[0:s_len, :] = v_ref[0].astype(BF16)
        k_s[s_len:, :] = kc_ref[0].astype(BF16)
        v_s[s_len:, :] = vc_ref[0].astype(BF16)

    lp = lam_ref[...]
    lam = (jnp.exp(jnp.sum(lp[0:1] * lp[1:2], axis=-1, keepdims=True))
           - jnp.exp(jnp.sum(lp[2:3] * lp[3:4], axis=-1, keepdims=True)) + lambda_init)
    low = _iota((1, LANES), 1) < DIFF_DH
    tq = q_ref.shape[1]
    part = tq // DIFF_Q_PARTS
    scores = []
    for i in range(DIFF_Q_PARTS):
        qs = q_ref[0, i * part:(i + 1) * part, :] * scale
        scores += [_mm_nt(jnp.where(low, qs, 0.0), k_s[...]), _mm_nt(jnp.where(low, 0.0, qs), k_s[...])]
    probs = [jnp.exp(s - jnp.max(s, axis=-1, keepdims=True)) for s in scores]
    outs = [_mm(p, v_s[...]) / jnp.sum(p, axis=-1, keepdims=True) for p in probs]
    for i in range(DIFF_Q_PARTS):
        o = outs[2 * i] - lam * outs[2 * i + 1]
        on = o * lax.rsqrt(jnp.mean(o * o, axis=-1, keepdims=True) + RMS_EPS) * ng_ref[...]
        y_ref[0, i * part:(i + 1) * part, :] = on * (1.0 - lambda_init)


def _diff(u, uc, lam_params, norm_g, col0, lambda_init, *, tq):
    bsz, s, _ = u.shape
    lc = uc.shape[1]
    qb = col0 // LANES
    kb, vb = qb + DIFF_HEADS, qb + 2 * DIFF_HEADS
    full = lambda arr_len, cb: pl.BlockSpec((1, arr_len, LANES), lambda b, h, j: (b, 0, cb + h))
    return pl.pallas_call(
        functools.partial(_diff_kernel, lambda_init=lambda_init),
        grid=(bsz, DIFF_HEADS, s // tq),
        in_specs=[pl.BlockSpec((1, tq, LANES), lambda b, h, j: (b, j, qb + h)),
                  full(s, kb), full(s, vb), full(lc, kb), full(lc, vb),
                  pl.BlockSpec((4, DIFF_DH), lambda b, h, j: (0, 0)),
                  pl.BlockSpec((1, DIFF_DV), lambda b, h, j: (0, 0))],
        out_specs=pl.BlockSpec((1, tq, LANES), lambda b, h, j: (b, j, h)),
        out_shape=jax.ShapeDtypeStruct((bsz, s, DIFF_HEADS * DIFF_DV), F32),
        scratch_shapes=[pltpu.VMEM((s + lc, LANES), BF16), pltpu.VMEM((s + lc, LANES), BF16)],
        compiler_params=_cparams("parallel", "parallel", "arbitrary"),
        name="diff_attn",
    )(u, u, u, uc, uc, lam_params, norm_g)


def _head_selector(width, head_dim, value):
    idx = jnp.arange(width) // head_dim
    return jnp.where(idx[:, None] == idx[None, :], value, 0.0).astype(F32)


def _pad_rows(w, lo, total):
    return jnp.zeros((total, w.shape[1]), w.dtype).at[lo:lo + w.shape[0]].set(w)


def _rope_tables(s, reps):
    pos = jnp.arange(s)
    n = DIFF_DH // 4
    freqs = ROPE_BASE ** (-jnp.arange(n, dtype=F32) / n)
    ang_r = (pos // GRID_W).astype(F32)[:, None] * freqs[None, :]
    ang_c = (pos % GRID_W).astype(F32)[:, None] * freqs[None, :]
    cos = jnp.concatenate([jnp.cos(ang_r)] * 2 + [jnp.cos(ang_c)] * 2, axis=-1)
    sin = jnp.concatenate([-jnp.sin(ang_r), jnp.sin(ang_r), -jnp.sin(ang_c), jnp.sin(ang_c)], axis=-1)
    return jnp.tile(cos, (1, reps)), jnp.tile(sin, (1, reps))


def _rwkv_na_mixer(xl, xc, mods, ctx_row, w_in, mu, w0, w2, a0, a2, g2, k_k, k_a, r_k, gn_g, gn_b, rpb):
    bsz, s, _ = xl.shape
    lc = xc.shape[1]
    n = w_in.shape[1]
    chunks = tuple((lo, min(lo + 512, n)) for lo in range(0, n, 512))
    u = _proj(xl, mods, w_in, chunks, tm=512)
    uc = _proj(xc, mods, w_in, chunks, tm=lc, ctx_row=ctx_row)

    eh = _head_selector(RWKV_WIDTH, RWKV_HEAD_DIM, 1.0).astype(BF16)
    em = _head_selector(RWKV_WIDTH, RWKV_HEAD_DIM, 1.0 / RWKV_HEAD_DIM).astype(BF16)
    p = dict(mu=mu[None], w0=w0, a0=a0, g2=g2.astype(BF16), k_k=k_k[None], k_a=k_a[None], r_k=r_k.reshape(1, -1), eh=eh,
             w2=jnp.stack([_pad_rows(w2[0], 0, 128), _pad_rows(w2[1], 64, 128)]).astype(BF16),
             a2=jnp.stack([_pad_rows(a2[0], 0, 128), _pad_rows(a2[1], 64, 128)]).astype(BF16))
    lanes = bsz * RWKV_HEADS
    state = jnp.zeros((2, RWKV_HEAD_DIM * RWKV_HEAD_DIM, lanes), F32)
    parts = []
    for uu, tt in ((uc, lc), (u, 256)):
        *scan_in, bonus, gate = _rwkv_prep(uu, p, tt=tt)
        y2, state = _rwkv_scan(*[_to_time_major(z) for z in scan_in], state)
        parts.append((_from_time_major(y2, bsz), bonus, gate, gn_g[None], gn_b[None], em))
    yc_a, y_a = parts
    y_b, yc_b = _na(u, uc, _na_bias_table(rpb, s // GRID_W), RWKV_IN)
    return (y_a, y_b), (yc_a, yc_b)


def _gla_diff_mixer(xl, xc, mods, ctx_row, w_in, gla_g2, gla_gb, gla_norm_g, diff_lambda, diff_norm_g, lambda_init):
    bsz, s, d = xl.shape
    lc = xc.shape[1]
    n_gla = 2 * GLA_HEADS * GLA_DK + 2 * GLA_HEADS * GLA_DV
    n_gate = 2 * GLA_GATE_LORA
    n_diff = 2 * DIFF_HEADS * 2 * DIFF_DH + DIFF_HEADS * DIFF_DV
    w_perm = jnp.concatenate([w_in[:, :n_gla], w_in[:, n_gla + n_gate:n_gla + n_gate + n_diff],
                              w_in[:, n_gla:n_gla + n_gate], jnp.zeros((d, LANES - n_gate), w_in.dtype)], axis=1).astype(BF16)
    rope_w = 2 * DIFF_HEADS * 2 * DIFF_DH
    chunks = ((0, 512), (512, 1024), (1024, n_gla), (n_gla, n_gla + rope_w),
              (n_gla + rope_w, n_gla + n_diff), (n_gla + n_diff, n_gla + n_diff + LANES))
    cos_t, sin_t = _rope_tables(s, rope_w // DIFF_DH)
    u = _proj(xl, mods, w_perm, chunks, tm=512, rope=(3, cos_t, sin_t))
    uc = _proj(xc, mods, w_perm, chunks, tm=lc, ctx_row=ctx_row)

    pairs = GLA_HEADS // 2
    g2p = jnp.stack([jnp.stack([_pad_rows(gla_g2[dd][:, p * LANES:(p + 1) * LANES], dd * GLA_GATE_LORA, LANES)
                                for p in range(pairs)]) for dd in range(2)]).astype(BF16)
    gbp = gla_gb.reshape(2, pairs, 1, LANES)
    y_c = _gla(u, uc, g2p, gbp, gla_norm_g[None], (n_gla + n_diff) // LANES)
    y_d = _diff(u, uc, diff_lambda, diff_norm_g[None], n_gla, lambda_init, tq=1024)
    return y_c, y_d


def kernel(x, c, ctx, c_ctx, w_mod_0, b_mod_0, ln_g_0, ln_b_0, ffn_gu_0, ffn_down_0, w_in_0, w_out_0, rwkv_mu_0, rwkv_w0_0, rwkv_w2_0, rwkv_a0_0, rwkv_a2_0, rwkv_g2_0, rwkv_k_k_0, rwkv_k_a_0, rwkv_r_k_0, rwkv_gn_g_0, rwkv_gn_b_0, na_rpb_0, w_mod_1, b_mod_1, ln_g_1, ln_b_1, ffn_gu_1, ffn_down_1, w_in_1, w_out_1, gla_g2_1, gla_gb_1, gla_norm_g_1, diff_lambda_1, diff_norm_g_1):
    bsz, s, d = x.shape
    lc = ctx.shape[1]
    ctx_row = bsz
    rows = -(-(bsz + 1) // SUBLANES) * SUBLANES
    c_all = jnp.concatenate([c, c_ctx[None], jnp.zeros((rows - bsz - 1, d), F32)], axis=0)
    tm = 512
    tf = 512

    mods = _mods(c_all, w_mod_0, b_mod_0)
    gu, dn = ffn_gu_0.astype(BF16), ffn_down_0.astype(BF16)
    xl = _ffn(x, mods, 0, gu[0], dn[0], ln_g_0[0], ln_b_0[0], tm=tf)
    xc = _ffn(ctx, mods, 0, gu[0], dn[0], ln_g_0[0], ln_b_0[0], tm=lc, ctx_row=ctx_row)
    (y_a, y_b), (yc_a, yc_b) = _rwkv_na_mixer(xl, xc, mods, ctx_row, w_in_0.astype(BF16), rwkv_mu_0, rwkv_w0_0, rwkv_w2_0,
                                              rwkv_a0_0, rwkv_a2_0, rwkv_g2_0, rwkv_k_k_0, rwkv_k_a_0, rwkv_r_k_0,
                                              rwkv_gn_g_0, rwkv_gn_b_0, na_rpb_0)
    wo = w_out_0.astype(BF16)
    xl = _rwkv_mixout(xl, mods, *y_a, y_b, wo, ln_g_0[1], ln_b_0[1], tm=tm)
    xc = _rwkv_mixout(xc, mods, *yc_a, yc_b, wo, ln_g_0[1], ln_b_0[1], tm=lc, ctx_row=ctx_row)
    xl = _ffn(xl, mods, 6, gu[1], dn[1], ln_g_0[2], ln_b_0[2], tm=tf)
    xc = _ffn(xc, mods, 6, gu[1], dn[1], ln_g_0[2], ln_b_0[2], tm=lc, ctx_row=ctx_row)

    mods = _mods(c_all, w_mod_1, b_mod_1)
    gu, dn = ffn_gu_1.astype(BF16), ffn_down_1.astype(BF16)
    xl = _ffn(xl, mods, 0, gu[0], dn[0], ln_g_1[0], ln_b_1[0], tm=tf)
    xc = _ffn(xc, mods, 0, gu[0], dn[0], ln_g_1[0], ln_b_1[0], tm=lc, ctx_row=ctx_row)
    y_c, y_d = _gla_diff_mixer(xl, xc, mods, ctx_row, w_in_1, gla_g2_1, gla_gb_1, gla_norm_g_1, diff_lambda_1, diff_norm_g_1,
                               0.8 - 0.6 * math.exp(-0.3 * 1))
    xl = _mixout(xl, mods, y_c, y_d, w_out_1.astype(BF16), ln_g_1[1], ln_b_1[1], tm=tm)
    xl = _ffn(xl, mods, 6, gu[1], dn[1], ln_g_1[2], ln_b_1[2], tm=tf)
    return xl
```
